```python
import math, functools
import jax, jax.numpy as jnp
from jax import lax
import numpy as np

D_MODEL = 2048
BATCH = 2
SEQ = 4096
DEPTH = 1
DEC_BATCH = 32
DEC_SEQ = 8
PAST_LEN = 8192
PAGE_SIZE = 128

HEAD_DIM = 128
H_FOX = 8
H_GDN = 8
D_FOX = H_FOX * HEAD_DIM
D_GDN = H_GDN * HEAD_DIM
D_MIX = D_FOX + D_GDN
CONV_W = 4
GDN_CHUNK = 64
Q_BLOCK = 128
N_EXPERTS = 32
TOP_K = 4
D_FF = D_MODEL
SWIGLU_ALPHA = 1.702
SWIGLU_LIMIT = 7.0
MOE_BLOCK = 128
RMS_EPS = 1e-6
N_ADA = 6
PROJ_SIZES = (D_FOX, D_FOX, D_FOX, H_FOX, D_GDN, D_GDN, D_GDN, D_GDN, H_GDN, H_GDN)
D_PROJ = sum(PROJ_SIZES)

kernel_name = 'hymba_fox_gdn_moe_adaln_step'


def rms_norm(x, w):
    xf = x.astype(jnp.float32)
    y = xf * lax.rsqrt(jnp.mean(xf * xf, axis=-1, keepdims=True) + RMS_EPS)
    return (y * w.astype(jnp.float32)).astype(x.dtype)


def l2_norm(x):
    xf = x.astype(jnp.float32)
    return xf * lax.rsqrt(jnp.sum(xf * xf, axis=-1, keepdims=True) + 1e-6)


def split_projection(p):
    offs = np.cumsum((0,) + PROJ_SIZES)
    return [p[..., int(offs[i]):int(offs[i + 1])] for i in range(len(PROJ_SIZES))]


def causal_conv(x, buf, w):
    T = x.shape[1]
    xp = jnp.concatenate([buf.astype(x.dtype), x], axis=1)
    y = sum(xp[:, i:i + T] * w[i] for i in range(CONV_W))
    return jax.nn.silu(y), xp[:, -(CONV_W - 1):]


def fox_prompt(q, k, v, logf):
    B, S, H, Dh = q.shape
    nb = S // Q_BLOCK
    scale = Dh ** -0.5
    F = jnp.cumsum(logf, axis=1)
    Fk = F.transpose(0, 2, 1)
    pos = jnp.arange(S)
    qb = q.reshape(B, nb, Q_BLOCK, H, Dh).swapaxes(0, 1)
    Fb = F.reshape(B, nb, Q_BLOCK, H).swapaxes(0, 1)
    pb = pos.reshape(nb, Q_BLOCK)

    def block(args):
        q_i, F_i, p_i = args
        s = jnp.einsum('bqhd,bkhd->bhqk', q_i, k, preferred_element_type=jnp.float32) * scale
        s = s + F_i.transpose(0, 2, 1)[..., None] - Fk[:, :, None, :]
        s = jnp.where(pos[None, None, None, :] <= p_i[None, None, :, None], s, -jnp.inf)
        p = jax.nn.softmax(s, axis=-1).astype(v.dtype)
        return jnp.einsum('bhqk,bkhd->bqhd', p, v)

    o = lax.map(block, (qb, Fb, pb))
    return o.swapaxes(0, 1).reshape(B, S, H, Dh)


def fox_sample(q, k, v, logf, k_past, v_past, logf_past):
    T = q.shape[1]
    P = k_past.shape[1]
    scale = q.shape[-1] ** -0.5
    F_past = jnp.cumsum(logf_past.astype(jnp.float32), axis=1)
    F_new = F_past[:, -1:] + jnp.cumsum(logf, axis=1)
    Fq = F_new.transpose(0, 2, 1)[..., None]
    s_past = jnp.einsum('bqhd,bkhd->bhqk', q, k_past, preferred_element_type=jnp.float32) * scale
    s_past = s_past + Fq - F_past.transpose(0, 2, 1)[:, :, None, :]
    s_new = jnp.einsum('bqhd,bkhd->bhqk', q, k, preferred_element_type=jnp.float32) * scale
    s_new = s_new + Fq - F_new.transpose(0, 2, 1)[:, :, None, :]
    s_new = jnp.where(jnp.tril(jnp.ones((T, T), bool)), s_new, -jnp.inf)
    p = jax.nn.softmax(jnp.concatenate([s_past, s_new], axis=-1), axis=-1).astype(v.dtype)
    return (jnp.einsum('bhqk,bkhd->bqhd', p[..., :P], v_past)
            + jnp.einsum('bhqk,bkhd->bqhd', p[..., P:], v))


def gated_delta_rule(q, k, v, g, beta, S0):
    B, T, H, Dk = q.shape
    Dv = v.shape[-1]
    C = min(GDN_CHUNK, T)
    n = -(-T // C)
    pad = n * C - T

    def padt(a):
        return jnp.pad(a, [(0, 0), (0, pad)] + [(0, 0)] * (a.ndim - 2))

    def chunk(a):
        return jnp.moveaxis(a.reshape((B, n, C) + a.shape[2:]), (1, 3), (0, 2))

    qc, kc, vc, gc, bc = [chunk(padt(a)) for a in (q * Dk ** -0.5, k, v, g, beta)]
    gcum = jnp.cumsum(gc, axis=-1)
    tri = jnp.tril(jnp.ones((C, C), bool))
    strict = jnp.tril(jnp.ones((C, C), bool), -1)
    diff = gcum[..., :, None] - gcum[..., None, :]
    decay = jnp.where(tri, jnp.exp(jnp.where(tri, diff, 0.0)), 0.0)
    kb = kc * bc[..., None]
    L = jnp.where(strict, jnp.einsum('nbhid,nbhjd->nbhij', kb, kc) * decay, 0.0)
    eye = jnp.eye(C, dtype=L.dtype)
    Tm = lax.linalg.triangular_solve(eye + L, jnp.broadcast_to(eye, L.shape), left_side=True, lower=True)
    u = Tm @ (vc * bc[..., None])
    w = Tm @ (kb * jnp.exp(gcum)[..., None])
    a_intra = jnp.where(tri, jnp.einsum('nbhid,nbhjd->nbhij', qc, kc) * decay, 0.0)

    def step(S, xs):
        q_i, k_i, u_i, w_i, g_i, a_i = xs
        v_new = u_i - w_i @ S
        o = (q_i * jnp.exp(g_i)[..., None]) @ S + a_i @ v_new
        g_last = g_i[..., -1:]
        S = S * jnp.exp(g_last)[..., None] + jnp.einsum(
            'bhck,bhcv->bhkv', k_i * jnp.exp(g_last - g_i)[..., None], v_new)
        return S, o

    S, o = lax.scan(step, S0, (qc, kc, u, w, gcum, a_intra))
    o = jnp.moveaxis(o, (0, 2), (1, 3)).reshape(B, n * C, H, Dv)[:, :T]
    return o, S


def clamped_swiglu(gu):
    glu, lin = jnp.split(gu, 2, axis=-1)
    glu = jnp.minimum(glu, SWIGLU_LIMIT)
    lin = jnp.clip(lin, -SWIGLU_LIMIT, SWIGLU_LIMIT)
    return glu * jax.nn.sigmoid(SWIGLU_ALPHA * glu) * (lin + 1.0)


def moe_ffn(h, w_router, b_router, w_gu, b_gu, w_down, b_down):
    N, D = h.shape
    logits = jnp.dot(h, w_router, preferred_element_type=jnp.float32) + b_router.astype(jnp.float32)
    top_val, top_idx = lax.top_k(logits, TOP_K)
    gate = jax.nn.softmax(top_val, axis=-1)
    A = N * TOP_K
    flat_e = top_idx.reshape(A)
    order = jnp.argsort(flat_e)
    sorted_e = flat_e[order]
    counts = jnp.bincount(flat_e, length=N_EXPERTS)
    starts = jnp.cumsum(counts) - counts
    padded = -(-counts // MOE_BLOCK) * MOE_BLOCK
    pad_ends = jnp.cumsum(padded)
    dest = (pad_ends - padded)[sorted_e] + jnp.arange(A) - starts[sorted_e]
    n_blocks = -(-(A + N_EXPERTS * (MOE_BLOCK - 1)) // MOE_BLOCK)
    src_tok = order // TOP_K
    buf = jnp.zeros((n_blocks * MOE_BLOCK, D), h.dtype).at[dest].set(h[src_tok])
    block_e = jnp.minimum(jnp.searchsorted(pad_ends, jnp.arange(n_blocks) * MOE_BLOCK, side='right'),
                          N_EXPERTS - 1)

    def expert_block(args):
        xb, e = args
        return clamped_swiglu(xb @ w_gu[e] + b_gu[e]) @ w_down[e] + b_down[e]

    out = lax.map(expert_block, (buf.reshape(n_blocks, MOE_BLOCK, D), block_e)).reshape(-1, D)
    contrib = out[dest].astype(jnp.float32) * gate.reshape(A)[order][:, None]
    return jnp.zeros((N, D), jnp.float32).at[src_tok].add(contrib).astype(h.dtype)


def layer_forward(x, c, fox_fn, conv_buf, S0, w_ada, b_ada, rms_mix, rms_ffn, w_in, b_forget,
                  q_norm, k_norm, conv_w, a_log, dt_bias, gdn_norm, w_out,
                  w_router, b_router, w_gu, b_gu, w_down, b_down):
    B, T, D = x.shape
    mod = jax.nn.silu(c) @ w_ada + b_ada
    sh1, sc1, g1, sh2, sc2, g2 = jnp.split(mod[:, None, :], N_ADA, axis=-1)
    h = rms_norm(x, rms_mix) * (1.0 + sc1) + sh1
    fq, fk, fv, ff, gq, gk, gv, gz, ga, gb = split_projection(h @ w_in)
    fq = rms_norm(fq.reshape(B, T, H_FOX, HEAD_DIM), q_norm)
    fk = rms_norm(fk.reshape(B, T, H_FOX, HEAD_DIM), k_norm)
    fv = fv.reshape(B, T, H_FOX, HEAD_DIM)
    logf = jax.nn.log_sigmoid(ff.astype(jnp.float32) + b_forget.astype(jnp.float32))
    o_fox = fox_fn(fq, fk, fv, logf).reshape(B, T, D_FOX)
    qkv, conv_state = causal_conv(jnp.concatenate([gq, gk, gv], axis=-1), conv_buf, conv_w)
    cq, ck, cv = jnp.split(qkv, 3, axis=-1)
    cq = l2_norm(cq.reshape(B, T, H_GDN, HEAD_DIM))
    ck = l2_norm(ck.reshape(B, T, H_GDN, HEAD_DIM))
    cv = cv.reshape(B, T, H_GDN, HEAD_DIM).astype(jnp.float32)
    beta = jax.nn.sigmoid(gb.astype(jnp.float32))
    g = -jnp.exp(a_log.astype(jnp.float32)) * jax.nn.softplus(ga.astype(jnp.float32) + dt_bias.astype(jnp.float32))
    o_gdn, S = gated_delta_rule(cq, ck, cv, g, beta, S0)
    o_gdn = rms_norm(o_gdn, gdn_norm) * jax.nn.silu(gz.reshape(B, T, H_GDN, HEAD_DIM).astype(jnp.float32))
    mix = jnp.concatenate([o_fox, o_gdn.reshape(B, T, D_GDN).astype(x.dtype)], axis=-1) @ w_out
    x = x + g1 * mix
    h2 = rms_norm(x, rms_ffn) * (1.0 + sc2) + sh2
    x = x + g2 * moe_ffn(h2.reshape(B * T, D), w_router, b_router, w_gu, b_gu, w_down, b_down).reshape(B, T, D)
    return x, (fk, fv, logf, S, conv_state)


def setup_inputs(seed: int = 0) -> dict:
    key = jax.random.key(seed)
    ks = jax.random.split(key, 32)
    f32 = jnp.float32

    def nrm(i, shape, s):
        return jax.random.normal(ks[i], shape, f32) * s

    n_pages = PAST_LEN // PAGE_SIZE
    n_used = DEC_BATCH * n_pages
    n_pool = n_used + max(1, n_used // 4)
    page_table = jax.random.permutation(ks[7], n_pool)[:n_used].reshape(DEC_BATCH, n_pages).astype(jnp.int32)
    dt = jnp.exp(jax.random.uniform(ks[20], (DEPTH, H_GDN), f32, math.log(1e-3), math.log(1e-1)))
    return {
        'x_prompt': nrm(0, (BATCH, SEQ, D_MODEL), 1.0),
        'x_sample': nrm(1, (DEC_BATCH, DEC_SEQ, D_MODEL), 1.0),
        'cache_k': nrm(2, (DEPTH, n_pool, PAGE_SIZE, H_FOX, HEAD_DIM), 1.0),
        'cache_v': nrm(3, (DEPTH, n_pool, PAGE_SIZE, H_FOX, HEAD_DIM), 1.0),
        'cache_logf': jax.nn.log_sigmoid(2.0 + nrm(4, (DEPTH, n_pool, PAGE_SIZE, H_FOX), 1.0)),
        'state_gdn': nrm(5, (DEPTH, DEC_BATCH, H_GDN, HEAD_DIM, HEAD_DIM), 0.3),
        'state_conv': nrm(6, (DEPTH, DEC_BATCH, CONV_W - 1, 3 * D_GDN), 1.0),
        'page_table': page_table,
        'c_prompt': nrm(8, (BATCH, D_MODEL), 1.0),
        'c_sample': nrm(9, (DEC_BATCH, D_MODEL), 1.0),
        'w_ada': nrm(10, (DEPTH, D_MODEL, N_ADA * D_MODEL), 0.5 * D_MODEL ** -0.5),
        'b_ada': nrm(11, (DEPTH, N_ADA * D_MODEL), 0.02),
        'rms_mix': 1.0 + nrm(12, (DEPTH, D_MODEL), 0.1),
        'rms_ffn': 1.0 + nrm(13, (DEPTH, D_MODEL), 0.1),
        'w_in': nrm(14, (DEPTH, D_MODEL, D_PROJ), D_MODEL ** -0.5),
        'b_forget': 2.0 + nrm(15, (DEPTH, H_FOX), 0.5),
        'q_norm': 1.0 + nrm(16, (DEPTH, HEAD_DIM), 0.1),
        'k_norm': 1.0 + nrm(17, (DEPTH, HEAD_DIM), 0.1),
        'conv_w': nrm(18, (DEPTH, CONV_W, 3 * D_GDN), CONV_W ** -0.5),
        'a_log': jnp.log(jax.random.uniform(ks[19], (DEPTH, H_GDN), f32, 1.0, 16.0)),
        'dt_bias': dt + jnp.log(-jnp.expm1(-dt)),
        'gdn_norm': 1.0 + nrm(21, (DEPTH, HEAD_DIM), 0.1),
        'w_out': nrm(22, (DEPTH, D_MIX, D_MODEL), D_MIX ** -0.5),
        'w_router': nrm(23, (DEPTH, D_MODEL, N_EXPERTS), D_MODEL ** -0.5),
        'b_router': nrm(24, (DEPTH, N_EXPERTS), 0.01),
        'w_gu': nrm(25, (DEPTH, N_EXPERTS, D_MODEL, 2 * D_FF), D_MODEL ** -0.5),
        'b_gu': nrm(26, (DEPTH, N_EXPERTS, 2 * D_FF), 0.02),
        'w_down': nrm(27, (DEPTH, N_EXPERTS, D_FF, D_MODEL), D_FF ** -0.5),
        'b_down': nrm(28, (DEPTH, N_EXPERTS, D_MODEL), 0.02),
    }


def reference(x_prompt, x_sample, cache_k, cache_v, cache_logf, state_gdn, state_conv, page_table,
              c_prompt, c_sample, w_ada, b_ada, rms_mix, rms_ffn, w_in, b_forget, q_norm, k_norm,
              conv_w, a_log, dt_bias, gdn_norm, w_out, w_router, b_router, w_gu, b_gu, w_down, b_down):
    Bp = x_prompt.shape[0]
    Bd = x_sample.shape[0]
    past = page_table.shape[1] * PAGE_SIZE
    yp, ys = x_prompt, x_sample
    outs = [[] for _ in range(10)]
    for l in range(DEPTH):
        params = (w_ada[l], b_ada[l], rms_mix[l], rms_ffn[l], w_in[l], b_forget[l], q_norm[l], k_norm[l],
                  conv_w[l], a_log[l], dt_bias[l], gdn_norm[l], w_out[l], w_router[l], b_router[l],
                  w_gu[l], b_gu[l], w_down[l], b_down[l])
        conv0 = jnp.zeros((Bp, CONV_W - 1, 3 * D_GDN), x_prompt.dtype)
        S0 = jnp.zeros((Bp, H_GDN, HEAD_DIM, HEAD_DIM), jnp.float32)
        yp, new_p = layer_forward(yp, c_prompt, fox_prompt, conv0, S0, *params)
        k_past = cache_k[l, page_table].reshape(Bd, past, H_FOX, HEAD_DIM)
        v_past = cache_v[l, page_table].reshape(Bd, past, H_FOX, HEAD_DIM)
        lf_past = cache_logf[l, page_table].reshape(Bd, past, H_FOX)
        fox_fn = functools.partial(fox_sample, k_past=k_past, v_past=v_past, logf_past=lf_past)
        ys, new_s = layer_forward(ys, c_sample, fox_fn, state_conv[l], state_gdn[l].astype(jnp.float32), *params)
        for lst, a in zip(outs, new_p + new_s):
            lst.append(a)
    k_p, v_p, lf_p, S_p, cv_p, k_s, v_s, lf_s, S_s, cv_s = [jnp.stack(a, axis=0) for a in outs]
    return (yp, ys, k_p, v_p, lf_p, S_p, cv_p, k_s, v_s, lf_s, S_s, cv_s)
```

```python
import functools

import jax
import jax.numpy as jnp
from jax import lax
from jax.experimental import pallas as pl
from jax.experimental.pallas import tpu as pltpu

F32 = jnp.float32
BF16 = jnp.bfloat16

D_MODEL = 2048
HEAD_DIM = 128
N_HEADS = 8
D_HEADS = N_HEADS * HEAD_DIM
CONV_W = 4
PAGE_SIZE = 128
N_EXPERTS = 32
TOP_K = 4
N_ADA = 6
RMS_EPS = 1e-6
L2_EPS = 1e-6
SWIGLU_ALPHA = 1.702
SWIGLU_LIMIT = 7.0

LANES = 128
SUBLANES = 8
GDN_CHUNK = 128
MOE_ROWS = 256
NEG_BIG = -1e30

_ARB = "arbitrary"


def _cparams(n_axes, vmem_mb):
    return pltpu.CompilerParams(dimension_semantics=(_ARB,) * n_axes,
                                vmem_limit_bytes=vmem_mb * 1024 * 1024)


def _dot(a, b):
    return jnp.dot(a, b, preferred_element_type=F32)


def _dot_nt(a, b):
    return lax.dot_general(a, b, (((1,), (1,)), ((), ())), preferred_element_type=F32)


def _dot_tn(a, b):
    return lax.dot_general(a, b, (((0,), (0,)), ((), ())), preferred_element_type=F32)


def _softplus(z):
    return jnp.maximum(z, 0.0) + jnp.log(1.0 + jnp.exp(-jnp.abs(z)))


def _sigmoid(z):
    return 1.0 / (1.0 + jnp.exp(-z))


def _ada_kernel(c_ref, w_ref, b_ref, o_ref):
    c = c_ref[...]
    a = (c * _sigmoid(c)).astype(BF16)
    o_ref[...] = _dot(a, w_ref[...].astype(BF16)) + b_ref[...]


def ada_mod(c_all, w_ada, b_ada):
    m, d = c_all.shape
    n = w_ada.shape[1]
    tn = min(1024, n)
    return pl.pallas_call(
        _ada_kernel,
        grid=(n // tn,),
        in_specs=[pl.BlockSpec((m, d), lambda j: (0, 0)),
                  pl.BlockSpec((d, tn), lambda j: (0, j)),
                  pl.BlockSpec((1, tn), lambda j: (0, j))],
        out_specs=pl.BlockSpec((m, tn), lambda j: (0, j)),
        out_shape=jax.ShapeDtypeStruct((m, n), F32),
        compiler_params=_cparams(1, 40),
        name="ada_mod",
    )(c_all, w_ada, b_ada.reshape(1, n))


def _mod_norm_bf16(x_ref, sc_ref, sh_ref, rw_ref):
    x = x_ref[...]
    g, tm, d = x.shape
    ms = jnp.mean(x * x, axis=-1, keepdims=True)
    h = x * lax.rsqrt(ms + RMS_EPS) * rw_ref[...]
    h = h * (1.0 + sc_ref[...]) + sh_ref[...]
    return h.reshape(g * tm, d).astype(BF16)


def _per_head(p, fn):
    return jnp.concatenate([fn(p[:, h * HEAD_DIM:(h + 1) * HEAD_DIM]) for h in range(N_HEADS)], axis=-1)


def _rms_head(s):
    return s * lax.rsqrt(jnp.mean(s * s, axis=-1, keepdims=True) + RMS_EPS)


def _l2_head(s):
    return s * lax.rsqrt(jnp.sum(s * s, axis=-1, keepdims=True) + L2_EPS)


def _proj_fox_kernel(x_ref, sc_ref, sh_ref, rw_ref, w_ref, qn_ref, kn_ref, pv_ref,
                     q_ref, k_ref, v_ref, kb_ref, vb_ref, small_ref, cum_ref, carry_ref, *, seg):
    i = pl.program_id(1)
    g, tm, _ = x_ref.shape
    m = g * tm
    hb = _mod_norm_bf16(x_ref, sc_ref, sh_ref, rw_ref)

    qn = _per_head(_dot(hb, w_ref[:, 0:D_HEADS]), _rms_head) * qn_ref[...]
    q_ref[...] = (qn * HEAD_DIM ** -0.5).astype(BF16).reshape(g, tm, D_HEADS)
    kn = _per_head(_dot(hb, w_ref[:, D_HEADS:2 * D_HEADS]), _rms_head) * kn_ref[...]
    k_ref[...] = kn.reshape(g, tm, D_HEADS)
    kb_ref[...] = kn.astype(BF16).reshape(g, tm, D_HEADS)
    vv = _dot(hb, w_ref[:, 2 * D_HEADS:3 * D_HEADS])
    v_ref[...] = vv.reshape(g, tm, D_HEADS)
    vb_ref[...] = vv.astype(BF16).reshape(g, tm, D_HEADS)

    z = _dot(hb, w_ref[:, 3 * D_HEADS:3 * D_HEADS + LANES]) + pv_ref[0:1, :]
    lane = lax.broadcasted_iota(jnp.int32, (m, LANES), 1)
    sp = _softplus(z)
    logsig = z - sp
    gval = -jnp.exp(pv_ref[1:2, :]) * sp
    is_f = lane < N_HEADS
    is_g = (lane >= N_HEADS) & (lane < 2 * N_HEADS)
    is_b = (lane >= 2 * N_HEADS) & (lane < 3 * N_HEADS)
    small = jnp.where(is_f, logsig, jnp.where(is_g, gval, jnp.where(is_b, _sigmoid(z), 0.0)))
    small_ref[...] = small.reshape(g, tm, LANES)

    row = lax.broadcasted_iota(jnp.int32, (m, LANES), 0)
    pos = row & (tm - 1)
    rowmod = jnp.where(is_g, pos & (seg - 1), pos)
    c = small
    s = 1
    while s < tm:
        c = c + jnp.where(rowmod >= s, pltpu.roll(c, s, axis=0), 0.0)
        s *= 2
    lane3 = lax.broadcasted_iota(jnp.int32, (g, tm, LANES), 2)
    carry = jnp.where(i == 0, 0.0, carry_ref[...])
    c3 = c.reshape(g, tm, LANES) + jnp.where(lane3 < N_HEADS, carry, 0.0)
    cum_ref[...] = c3
    carry_ref[...] = c3[:, tm - 1:tm, :]


def proj_fox(x3, sc, sh, rms_w, w_fox, qn_t, kn_t, pvec, *, group, tm):
    nseq, t, d = x3.shape
    g = group
    seg = min(GDN_CHUNK, tm)
    grid = (nseq // g, t // tm)
    blk = lambda w: pl.BlockSpec((g, tm, w), lambda s, i: (s, i, 0))
    const2 = lambda a: pl.BlockSpec(a.shape, lambda s, i: (0, 0))
    mod = pl.BlockSpec((g, 1, d), lambda s, i: (s, 0, 0))
    outs = [jax.ShapeDtypeStruct((nseq, t, D_HEADS), BF16),
            jax.ShapeDtypeStruct((nseq, t, D_HEADS), F32),
            jax.ShapeDtypeStruct((nseq, t, D_HEADS), F32),
            jax.ShapeDtypeStruct((nseq, t, D_HEADS), BF16),
            jax.ShapeDtypeStruct((nseq, t, D_HEADS), BF16),
            jax.ShapeDtypeStruct((nseq, t, LANES), F32),
            jax.ShapeDtypeStruct((nseq, t, LANES), F32)]
    return pl.pallas_call(
        functools.partial(_proj_fox_kernel, seg=seg),
        grid=grid,
        in_specs=[blk(d), mod, mod, const2(rms_w),
                  pl.BlockSpec(w_fox.shape, lambda s, i: (0, 0), pipeline_mode=pl.Buffered(1)),
                  const2(qn_t), const2(kn_t), const2(pvec)],
        out_specs=[blk(D_HEADS)] * 5 + [blk(LANES)] * 2,
        out_shape=outs,
        scratch_shapes=[pltpu.VMEM((g, 1, LANES), F32)],
        compiler_params=_cparams(2, 48),
        name="proj_fox",
    )(x3, sc, sh, rms_w, w_fox, qn_t, kn_t, pvec)


def _proj_gdn_kernel(x_ref, sc_ref, sh_ref, rw_ref, w_ref, cw_ref, cin_ref,
                     q_ref, k_ref, v_ref, z_ref, cst_ref, cbuf_ref):
    i = pl.program_id(1)
    g, tm, _ = x_ref.shape
    hb = _mod_norm_bf16(x_ref, sc_ref, sh_ref, rw_ref)
    outs = (q_ref, k_ref, v_ref)
    for c in range(3):
        cols = slice(c * D_HEADS, (c + 1) * D_HEADS)
        pc = _dot(hb, w_ref[:, cols]).reshape(g, tm, D_HEADS)

        @pl.when(i == 0)
        def _():
            cbuf_ref[c, :, 0:SUBLANES, :] = cin_ref[:, :, cols]

        @pl.when(i > 0)
        def _():
            cbuf_ref[c, :, 0:SUBLANES, :] = cbuf_ref[c, :, tm:tm + SUBLANES, :]

        cbuf_ref[c, :, SUBLANES:tm + SUBLANES, :] = pc
        cst_ref[:, :, cols] = pc[:, tm - SUBLANES:tm, :]
        base = SUBLANES - (CONV_W - 1)
        y = None
        for j in range(CONV_W):
            term = cbuf_ref[c, :, base + j:base + j + tm, :] * cw_ref[j:j + 1, cols]
            y = term if y is None else y + term
        y = (y * _sigmoid(y)).reshape(g * tm, D_HEADS)
        if c == 0:
            y = _per_head(y, _l2_head) * HEAD_DIM ** -0.5
        elif c == 1:
            y = _per_head(y, _l2_head)
        outs[c][...] = y.reshape(g, tm, D_HEADS)
    z_ref[...] = _dot(hb, w_ref[:, 3 * D_HEADS:4 * D_HEADS]).reshape(g, tm, D_HEADS)


def proj_gdn(x3, sc, sh, rms_w, w_g, conv_w, conv_in, *, group, tm):
    nseq, t, d = x3.shape
    g = group
    grid = (nseq // g, t // tm)
    blk = lambda w: pl.BlockSpec((g, tm, w), lambda s, i: (s, i, 0))
    const2 = lambda a: pl.BlockSpec(a.shape, lambda s, i: (0, 0))
    mod = pl.BlockSpec((g, 1, d), lambda s, i: (s, 0, 0))
    st = pl.BlockSpec((g, SUBLANES, 3 * D_HEADS), lambda s, i: (s, 0, 0))
    outs = [jax.ShapeDtypeStruct((nseq, t, D_HEADS), F32)] * 4 + \
           [jax.ShapeDtypeStruct((nseq, SUBLANES, 3 * D_HEADS), F32)]
    return pl.pallas_call(
        _proj_gdn_kernel,
        grid=grid,
        in_specs=[blk(d), mod, mod, const2(rms_w),
                  pl.BlockSpec(w_g.shape, lambda s, i: (0, 0), pipeline_mode=pl.Buffered(1)),
                  const2(conv_w), st],
        out_specs=[blk(D_HEADS)] * 4 + [st],
        out_shape=outs,
        scratch_shapes=[pltpu.VMEM((3, g, tm + SUBLANES, D_HEADS), F32)],
        compiler_params=_cparams(2, 52),
        name="proj_gdn",
    )(x3, sc, sh, rms_w, w_g, conv_w, conv_in)


def _fox_prompt_kernel(q_ref, k_ref, v_ref, fc_ref, fr_ref, o_ref, *, tq):
    qi = pl.program_id(2)
    q = q_ref[0]
    fq = fc_ref[0]

    def step(j, carry, masked):
        m_i, l_i, acc = carry
        off = pl.multiple_of(j * tq, tq)
        ks = k_ref[0, pl.ds(off, tq), :]
        vs = v_ref[0, pl.ds(off, tq), :]
        s = _dot_nt(q, ks) + (fq - fr_ref[0, :, pl.ds(off, tq)])
        if masked:
            r = lax.broadcasted_iota(jnp.int32, (tq, tq), 0)
            c = lax.broadcasted_iota(jnp.int32, (tq, tq), 1)
            s = jnp.where(c <= r, s, NEG_BIG)
        m_new = jnp.maximum(m_i, jnp.max(s, axis=-1, keepdims=True))
        alpha = jnp.exp(m_i - m_new)
        p = jnp.exp(s - m_new)
        l_new = alpha * l_i + jnp.sum(p, axis=-1, keepdims=True)
        acc = alpha * acc + _dot(p.astype(BF16), vs)
        return m_new, l_new, acc

    init = (jnp.full((tq, 1), NEG_BIG, F32), jnp.zeros((tq, 1), F32), jnp.zeros((tq, HEAD_DIM), F32))
    carry = lax.fori_loop(0, qi, functools.partial(step, masked=False), init)
    _, l_i, acc = step(qi, carry, True)
    o_ref[0] = (acc / l_i).astype(BF16)


def fox_prompt_attention(qb, kb, vb, f_col, f_row, *, tq):
    b, s, _ = qb.shape
    grid = (b, N_HEADS, s // tq)
    return pl.pallas_call(
        functools.partial(_fox_prompt_kernel, tq=tq),
        grid=grid,
        in_specs=[pl.BlockSpec((1, tq, HEAD_DIM), lambda bi, h, qi: (bi, qi, h)),
                  pl.BlockSpec((1, s, HEAD_DIM), lambda bi, h, qi: (bi, 0, h)),
                  pl.BlockSpec((1, s, HEAD_DIM), lambda bi, h, qi: (bi, 0, h)),
                  pl.BlockSpec((1, tq, 1), lambda bi, h, qi: (bi * N_HEADS + h, qi, 0)),
                  pl.BlockSpec((1, 1, s), lambda bi, h, qi: (bi * N_HEADS + h, 0, 0))],
        out_specs=pl.BlockSpec((1, tq, HEAD_DIM), lambda bi, h, qi: (bi, qi, h)),
        out_shape=jax.ShapeDtypeStruct((b, s, D_HEADS), BF16),
        compiler_params=_cparams(3, 32),
        name="fox_prompt_attention",
    )(qb, kb, vb, f_col, f_row)


def _logf_suffix_kernel(pt_ref, lf_ref, r_ref, carry_ref):
    s = pl.program_id(1)
    x = lf_ref[0]
    n = x.shape[0]
    row = lax.broadcasted_iota(jnp.int32, x.shape, 0)
    c = x
    k = 1
    while k < n:
        c = c + jnp.where(row + k < n, pltpu.roll(c, n - k, axis=0), 0.0)
        k *= 2
    carry = jnp.where(s == 0, 0.0, carry_ref[...])
    r_ref[0] = c - x + carry
    carry_ref[...] = carry + c[0:1, :]


def logf_suffix(cache_logf, page_table):
    n_pool, page, nh = cache_logf.shape
    bd, n_pages = page_table.shape
    gs = pltpu.PrefetchScalarGridSpec(
        num_scalar_prefetch=1,
        grid=(bd, n_pages),
        in_specs=[pl.BlockSpec((1, page, nh), lambda b, s, pt: (pt[b, n_pages - 1 - s], 0, 0))],
        out_specs=pl.BlockSpec((1, page, nh), lambda b, s, pt: (b, n_pages - 1 - s, 0)),
        scratch_shapes=[pltpu.VMEM((1, nh), F32)])
    return pl.pallas_call(
        _logf_suffix_kernel, grid_spec=gs,
        out_shape=jax.ShapeDtypeStruct((bd, n_pages * page, nh), F32),
        compiler_params=_cparams(2, 16),
        name="logf_suffix",
    )(page_table, cache_logf)


def _rows_per_head(a, reps):
    return jnp.concatenate([jnp.broadcast_to(a[h:h + 1, :], (reps, a.shape[1])) for h in range(N_HEADS)], axis=0)


def _fox_sample_kernel(pt_ref, q_ref, *refs, pp, t):
    k_refs = refs[0:pp]
    v_refs = refs[pp:2 * pp]
    r_ref, kn_ref, vn_ref, csr_ref, csc_ref, o_ref, qbd_ref, m_ref, l_ref, acc_ref = refs[2 * pp:]
    s = pl.program_id(1)
    ns = pl.num_programs(1)
    hq = N_HEADS * t

    @pl.when(s == 0)
    def _():
        q = q_ref[0]
        lane_head = lax.broadcasted_iota(jnp.int32, q.shape, 1) // HEAD_DIM
        qbd_ref[...] = jnp.concatenate(
            [jnp.where(lane_head == h, q, jnp.zeros_like(q)) for h in range(N_HEADS)], axis=0)
        m_ref[...] = jnp.full(m_ref.shape, NEG_BIG, F32)
        l_ref[...] = jnp.zeros(l_ref.shape, F32)
        acc_ref[...] = jnp.zeros(acc_ref.shape, F32)

    qbd = qbd_ref[...]
    csc = csc_ref[0]

    def update(sc, vals):
        m_old = m_ref[...]
        m_new = jnp.maximum(m_old, jnp.max(sc, axis=-1, keepdims=True))
        alpha = jnp.exp(m_old - m_new)
        p = jnp.exp(sc - m_new)
        l_ref[...] = alpha * l_ref[...] + jnp.sum(p, axis=-1, keepdims=True)
        pv = None
        for j, vv in enumerate(vals):
            term = _dot(p[:, j * PAGE_SIZE:(j + 1) * PAGE_SIZE].astype(BF16), vv)
            pv = term if pv is None else pv + term
        acc_ref[...] = alpha * acc_ref[...] + pv
        m_ref[...] = m_new

    sc = jnp.concatenate([_dot_nt(qbd, kr[0].astype(BF16)) for kr in k_refs], axis=1)
    sc = sc + (csc + _rows_per_head(r_ref[0], t))
    update(sc, [vr[0].astype(BF16) for vr in v_refs])

    @pl.when(s == ns - 1)
    def _():
        sn = _dot_nt(qbd, kn_ref[0]) + (csc - _rows_per_head(csr_ref[0], t))
        row = lax.broadcasted_iota(jnp.int32, (hq, PAGE_SIZE), 0)
        col = lax.broadcasted_iota(jnp.int32, (hq, PAGE_SIZE), 1)
        sn = jnp.where(col <= (row & (t - 1)), sn, NEG_BIG)
        update(sn, [vn_ref[0]])
        res = acc_ref[...] / l_ref[...]
        lane_head = lax.broadcasted_iota(jnp.int32, (t, D_HEADS), 1) // HEAD_DIM
        out = jnp.zeros((t, D_HEADS), F32)
        for h in range(N_HEADS):
            out = out + jnp.where(lane_head == h, res[h * t:(h + 1) * t, :], 0.0)
        o_ref[0] = out.astype(BF16)


def fox_sample_attention(qb, cache_k2, cache_v2, page_table, r_rows, kn_pad, vn_pad, cs_rows, cs_col, *, pp):
    bd, t, _ = qb.shape
    n_pages = page_table.shape[1]
    ns = n_pages // pp
    hq = N_HEADS * t
    page_spec = lambda j: pl.BlockSpec((1, PAGE_SIZE, D_HEADS), lambda b, s, pt: (pt[b, s * pp + j], 0, 0))
    per_b = lambda shp: pl.BlockSpec((1,) + shp, lambda b, s, pt: (b, 0, 0))
    gs = pltpu.PrefetchScalarGridSpec(
        num_scalar_prefetch=1,
        grid=(bd, ns),
        in_specs=[per_b((t, D_HEADS))] + [page_spec(j) for j in range(pp)] * 2 +
                 [pl.BlockSpec((1, N_HEADS, PAGE_SIZE * pp), lambda b, s, pt: (b, 0, s)),
                  per_b((PAGE_SIZE, D_HEADS)), per_b((PAGE_SIZE, D_HEADS)),
                  per_b((N_HEADS, PAGE_SIZE)), per_b((hq, 1))],
        out_specs=per_b((t, D_HEADS)),
        scratch_shapes=[pltpu.VMEM((hq, D_HEADS), BF16), pltpu.VMEM((hq, 1), F32),
                        pltpu.VMEM((hq, 1), F32), pltpu.VMEM((hq, D_HEADS), F32)])
    return pl.pallas_call(
        functools.partial(_fox_sample_kernel, pp=pp, t=t), grid_spec=gs,
        out_shape=jax.ShapeDtypeStruct((bd, t, D_HEADS), BF16),
        compiler_params=_cparams(2, 32),
        name="fox_sample_attention",
    )(page_table, qb, *([cache_k2] * pp), *([cache_v2] * pp), r_rows, kn_pad, vn_pad, cs_rows, cs_col)


def _gdn_kernel(q_ref, k_ref, v_ref, z_ref, small_ref, cum_ref, s0_ref, nw_ref, o_ref, so_ref, s_ref):
    ci = pl.program_id(1)
    c = GDN_CHUNK

    @pl.when(ci == 0)
    def _():
        s_ref[...] = s0_ref[0]

    cum = cum_ref[0]
    cum_t = cum.T
    small = small_ref[0]
    r = lax.broadcasted_iota(jnp.int32, (c, c), 0)
    cc = lax.broadcasted_iota(jnp.int32, (c, c), 1)
    tri = cc <= r
    strict = cc < r
    eye = (cc == r).astype(F32)
    pair_masks = []
    shift = 0
    while (1 << shift) < c:
        rb = r >> shift
        pair_masks.append(((rb & 1) == 1) & ((cc >> shift) == rb - 1))
        shift += 1
    for h in range(N_HEADS):
        hs = slice(h * HEAD_DIM, (h + 1) * HEAD_DIM)
        qh = q_ref[0, :, hs]
        kh = k_ref[0, :, hs]
        vh = v_ref[0, :, hs]
        gc = cum[:, N_HEADS + h:N_HEADS + h + 1]
        gc_row = cum_t[N_HEADS + h:N_HEADS + h + 1, :]
        bt = small[:, 2 * N_HEADS + h:2 * N_HEADS + h + 1]
        g_last = gc[c - 1:c, :]
        decay = jnp.where(tri, jnp.exp(jnp.where(tri, gc - gc_row, 0.0)), 0.0)
        kb = kh * bt
        kk_qk = _dot_nt(jnp.concatenate([kb, qh], axis=0).astype(BF16), kh.astype(BF16))
        lmat = jnp.where(strict, kk_qk[:c] * decay, 0.0)
        a_in = jnp.where(tri, kk_qk[c:] * decay, 0.0)
        tm = eye
        for e_mask in pair_masks:
            tb = tm.astype(BF16)
            te = _dot(tb, jnp.where(e_mask, lmat, 0.0).astype(BF16))
            tm = tm - _dot(te.astype(BF16), tb)
        eg = jnp.exp(gc)
        uw = _dot(tm.astype(BF16), jnp.concatenate([vh * bt, kb * eg], axis=1).astype(BF16))
        uwb = uw.astype(BF16)
        au = _dot(a_in.astype(BF16), uwb)
        kt = kh * jnp.exp(g_last - gc)
        ku = _dot(kt.T.astype(BF16), uwb)
        qt = qh * eg - au[:, HEAD_DIM:]
        s_h = s_ref[h]
        x = _dot(jnp.concatenate([-ku[:, HEAD_DIM:], qt], axis=0).astype(BF16), s_h.astype(BF16))
        s_ref[h] = s_h * jnp.exp(g_last) + x[:HEAD_DIM] + ku[:, :HEAD_DIM]
        o = x[HEAD_DIM:] + au[:, :HEAD_DIM]
        o = o * lax.rsqrt(jnp.mean(o * o, axis=-1, keepdims=True) + RMS_EPS) * nw_ref[...]
        zh = z_ref[0, :, hs]
        o_ref[0, :, hs] = (o * (zh * _sigmoid(zh))).astype(BF16)
    so_ref[0] = s_ref[...]


def gdn_chunks(q, k, v, z, small, cum, s0, norm_w):
    nseq, tp, _ = q.shape
    c = GDN_CHUNK
    blk = lambda w: pl.BlockSpec((1, c, w), lambda s, i: (s, i, 0))
    st = pl.BlockSpec((1, N_HEADS, HEAD_DIM, HEAD_DIM), lambda s, i: (s, 0, 0, 0))
    return pl.pallas_call(
        _gdn_kernel,
        grid=(nseq, tp // c),
        in_specs=[blk(D_HEADS)] * 4 + [blk(LANES)] * 2 + [st, pl.BlockSpec((1, HEAD_DIM), lambda s, i: (0, 0))],
        out_specs=[blk(D_HEADS), st],
        out_shape=[jax.ShapeDtypeStruct((nseq, tp, D_HEADS), BF16),
                   jax.ShapeDtypeStruct((nseq, N_HEADS, HEAD_DIM, HEAD_DIM), F32)],
        scratch_shapes=[pltpu.VMEM((N_HEADS, HEAD_DIM, HEAD_DIM), F32)],
        compiler_params=_cparams(2, 32),
        name="gdn_chunks",
    )(q, k, v, z, small, cum, s0, norm_w)


def _split3(a):
    hi = a.astype(BF16)
    lo = (a - hi.astype(F32)).astype(BF16)
    return hi, lo


def _out_proj_kernel(x_ref, of_ref, og_ref, g1_ref, sc_ref, sh_ref, rw_ref, wt_ref, wb_ref, wr_ref, br_ref,
                     x1_ref, h2_ref, lg_ref):
    g, tm, d = x_ref.shape
    m = g * tm
    mix = _dot(of_ref[...].reshape(m, D_HEADS), wt_ref[...]) + _dot(og_ref[...].reshape(m, D_HEADS), wb_ref[...])
    x1 = x_ref[...] + g1_ref[...] * mix.reshape(g, tm, d)
    x1_ref[...] = x1
    ms = jnp.mean(x1 * x1, axis=-1, keepdims=True)
    h2 = x1 * lax.rsqrt(ms + RMS_EPS) * rw_ref[...]
    h2 = (h2 * (1.0 + sc_ref[...]) + sh_ref[...]).reshape(m, d)
    h2_ref[...] = h2.astype(BF16).reshape(g, tm, d)
    hh, hl = _split3(h2)
    wh, wl = _split3(wr_ref[...])
    lg = _dot(hh, wh) + (_dot(hh, wl) + _dot(hl, wh)) + br_ref[...]
    lg_ref[...] = lg.reshape(g, tm, LANES)


def out_proj(x3, o_fox, o_gdn, g1, sc2, sh2, rms_w, w_top, w_bot, w_router_p, b_router_p, *, group, tm):
    nseq, t, d = x3.shape
    g = group
    blk = lambda w: pl.BlockSpec((g, tm, w), lambda s, i: (s, i, 0))
    const2 = lambda a: pl.BlockSpec(a.shape, lambda s, i: (0, 0))
    res = lambda a: pl.BlockSpec(a.shape, lambda s, i: (0, 0), pipeline_mode=pl.Buffered(1))
    mod = pl.BlockSpec((g, 1, d), lambda s, i: (s, 0, 0))
    return pl.pallas_call(
        _out_proj_kernel,
        grid=(nseq // g, t // tm),
        in_specs=[blk(d), blk(D_HEADS), blk(D_HEADS), mod, mod, mod, const2(rms_w),
                  res(w_top), res(w_bot), const2(w_router_p), const2(b_router_p)],
        out_specs=[blk(d), blk(d), blk(LANES)],
        out_shape=[jax.ShapeDtypeStruct((nseq, t, d), F32), jax.ShapeDtypeStruct((nseq, t, d), BF16),
                   jax.ShapeDtypeStruct((nseq, t, LANES), F32)],
        compiler_params=_cparams(2, 40),
        name="out_proj",
    )(x3, o_fox, o_gdn, g1, sc2, sh2, rms_w, w_top, w_bot, w_router_p, b_router_p)


def _moe_gu_kernel(be_ref, nu_ref, x_ref, wg_ref, wl_ref, bg_ref, bl_ref, h_ref, wgb_ref, wlb_ref):
    i = pl.program_id(1)

    @pl.when(i < nu_ref[0])
    def _():
        @pl.when((i == 0) | (be_ref[i] != be_ref[jnp.maximum(i - 1, 0)]))
        def _():
            wgb_ref[...] = wg_ref[0].astype(BF16)
            wlb_ref[...] = wl_ref[0].astype(BF16)

        x = x_ref[...]
        glu = jnp.minimum(_dot(x, wgb_ref[...]) + bg_ref[0], SWIGLU_LIMIT)
        lin = jnp.clip(_dot(x, wlb_ref[...]) + bl_ref[0], -SWIGLU_LIMIT, SWIGLU_LIMIT)
        h_ref[...] = (glu * _sigmoid(SWIGLU_ALPHA * glu) * (lin + 1.0)).astype(BF16)


def moe_gate_up(x_sorted, block_e, n_used, w_gu, b_gu3, *, tn):
    r, d = x_sorted.shape
    ne, _, two_ff = w_gu.shape
    d_ff = two_ff // 2
    nj = d_ff // tn
    nb = r // MOE_ROWS
    blk_i = lambda i, nu: jnp.minimum(i, nu[0] - 1)
    gs = pltpu.PrefetchScalarGridSpec(
        num_scalar_prefetch=2,
        grid=(nj, nb),
        in_specs=[pl.BlockSpec((MOE_ROWS, d), lambda j, i, be, nu: (blk_i(i, nu), 0)),
                  pl.BlockSpec((1, d, tn), lambda j, i, be, nu: (be[blk_i(i, nu)], 0, j)),
                  pl.BlockSpec((1, d, tn), lambda j, i, be, nu: (be[blk_i(i, nu)], 0, nj + j)),
                  pl.BlockSpec((1, 1, tn), lambda j, i, be, nu: (be[blk_i(i, nu)], 0, j)),
                  pl.BlockSpec((1, 1, tn), lambda j, i, be, nu: (be[blk_i(i, nu)], 0, nj + j))],
        out_specs=pl.BlockSpec((MOE_ROWS, tn), lambda j, i, be, nu: (blk_i(i, nu), j)),
        scratch_shapes=[pltpu.VMEM((d, tn), BF16), pltpu.VMEM((d, tn), BF16)])
    return pl.pallas_call(
        _moe_gu_kernel, grid_spec=gs,
        out_shape=jax.ShapeDtypeStruct((r, d_ff), BF16),
        compiler_params=_cparams(2, 40),
        name="moe_gate_up",
    )(block_e, n_used, x_sorted, w_gu, w_gu, b_gu3, b_gu3)


def _moe_down_kernel(be_ref, nu_ref, h_ref, w_ref, b_ref, gate_ref, o_ref, wb_ref):
    i = pl.program_id(1)

    @pl.when(i < nu_ref[0])
    def _():
        @pl.when((i == 0) | (be_ref[i] != be_ref[jnp.maximum(i - 1, 0)]))
        def _():
            wb_ref[...] = w_ref[0].astype(BF16)

        o_ref[...] = (_dot(h_ref[...], wb_ref[...]) + b_ref[0]) * gate_ref[...]


def moe_down(h_sorted, block_e, n_used, w_down, b_down3, row_gate, *, tn):
    r, d_ff = h_sorted.shape
    d = w_down.shape[2]
    nj = d // tn
    nb = r // MOE_ROWS
    blk_i = lambda i, nu: jnp.minimum(i, nu[0] - 1)
    gs = pltpu.PrefetchScalarGridSpec(
        num_scalar_prefetch=2,
        grid=(nj, nb),
        in_specs=[pl.BlockSpec((MOE_ROWS, d_ff), lambda j, i, be, nu: (blk_i(i, nu), 0)),
                  pl.BlockSpec((1, d_ff, tn), lambda j, i, be, nu: (be[blk_i(i, nu)], 0, j)),
                  pl.BlockSpec((1, 1, tn), lambda j, i, be, nu: (be[blk_i(i, nu)], 0, j)),
                  pl.BlockSpec((MOE_ROWS, 1), lambda j, i, be, nu: (blk_i(i, nu), 0))],
        out_specs=pl.BlockSpec((MOE_ROWS, tn), lambda j, i, be, nu: (blk_i(i, nu), j)),
        scratch_shapes=[pltpu.VMEM((d_ff, tn), BF16)])
    return pl.pallas_call(
        _moe_down_kernel, grid_spec=gs,
        out_shape=jax.ShapeDtypeStruct((r, d), F32),
        compiler_params=_cparams(2, 40),
        name="moe_down",
    )(block_e, n_used, h_sorted, w_down, b_down3, row_gate)


def _route(logits):
    n = logits.shape[0]
    a = n * TOP_K
    top_val, top_idx = lax.top_k(logits, TOP_K)
    gate = jax.nn.softmax(top_val, axis=-1)
    flat_e = top_idx.reshape(a)
    order = jnp.argsort(flat_e)
    sorted_e = flat_e[order]
    counts = jnp.bincount(flat_e, length=N_EXPERTS)
    starts = jnp.cumsum(counts) - counts
    padded = -(-counts // MOE_ROWS) * MOE_ROWS
    pad_ends = jnp.cumsum(padded)
    dest = (pad_ends - padded)[sorted_e] + jnp.arange(a) - starts[sorted_e]
    nb = -(-(a + N_EXPERTS * (MOE_ROWS - 1)) // MOE_ROWS)
    r = nb * MOE_ROWS
    row_tok = jnp.zeros((r,), jnp.int32).at[dest].set((order // TOP_K).astype(jnp.int32))
    row_gate = jnp.zeros((r,), F32).at[dest].set(gate.reshape(a)[order])
    block_e = jnp.minimum(jnp.searchsorted(pad_ends, jnp.arange(nb) * MOE_ROWS, side='right'),
                          N_EXPERTS - 1).astype(jnp.int32)
    n_used = (pad_ends[-1] // MOE_ROWS).astype(jnp.int32).reshape(1)
    pos = jnp.zeros((a,), jnp.int32).at[order].set(dest.astype(jnp.int32)).reshape(n, TOP_K)
    return row_tok, row_gate, block_e, n_used, pos


def _forward(x_prompt, x_sample, cache_k, cache_v, cache_logf, state_gdn, state_conv, page_table,
             c_prompt, c_sample, w_ada, b_ada, rms_mix, rms_ffn, w_in, b_forget, q_norm, k_norm,
             conv_w, a_log, dt_bias, gdn_norm, w_out, w_router, b_router, w_gu, b_gu, w_down, b_down):
    depth = w_ada.shape[0]
    assert depth == 1, "single-layer trunk"
    l = 0
    bp, s, d = x_prompt.shape
    bd, t, _ = x_sample.shape
    n_pool = cache_k.shape[1]
    tm_p = 256

    n_c = bp + bd
    c_all = jnp.concatenate([c_prompt, c_sample, jnp.zeros((-n_c % SUBLANES, d), F32)], axis=0)
    mod = ada_mod(c_all, w_ada[l], b_ada[l])
    modp = mod[:bp].reshape(bp, 1, N_ADA, d)
    mods = mod[bp:n_c].reshape(bd, 1, N_ADA, d)

    w = w_in[l]
    o_ff, o_g, o_ab = 3 * D_HEADS, 3 * D_HEADS + N_HEADS, 7 * D_HEADS + N_HEADS
    w_small = jnp.concatenate([w[:, o_ff:o_g], w[:, o_ab:o_ab + 2 * N_HEADS]], axis=1)
    w_small = jnp.pad(w_small, ((0, 0), (0, LANES - 3 * N_HEADS)))
    w_fox = jnp.concatenate([w[:, :o_ff], w_small], axis=1).astype(BF16)
    w_g = w[:, o_g:o_ab].astype(BF16)
    qn_t = jnp.tile(q_norm[l], N_HEADS).reshape(1, D_HEADS)
    kn_t = jnp.tile(k_norm[l], N_HEADS).reshape(1, D_HEADS)
    pvec = jnp.zeros((SUBLANES, LANES), F32)
    pvec = pvec.at[0, 0:N_HEADS].set(b_forget[l]).at[0, N_HEADS:2 * N_HEADS].set(dt_bias[l])
    pvec = pvec.at[1, N_HEADS:2 * N_HEADS].set(a_log[l])
    rw1 = rms_mix[l].reshape(1, d)
    rw2 = rms_ffn[l].reshape(1, d)
    w_top = w_out[l][:D_HEADS].astype(BF16)
    w_bot = w_out[l][D_HEADS:].astype(BF16)
    w_router_p = jnp.pad(w_router[l], ((0, 0), (0, LANES - N_EXPERTS)))
    b_router_p = jnp.pad(b_router[l], (0, LANES - N_EXPERTS)).reshape(1, LANES)
    nw = gdn_norm[l].reshape(1, HEAD_DIM)

    def mixer_inputs(x3, m4, conv_in, group, tm):
        sh1, sc1 = m4[:, :, 0], m4[:, :, 1]
        pf = proj_fox(x3, sc1, sh1, rw1, w_fox, qn_t, kn_t, pvec, group=group, tm=tm)
        pg = proj_gdn(x3, sc1, sh1, rw1, w_g, conv_w[l], conv_in, group=group, tm=tm)
        return pf, pg

    pf, pg = mixer_inputs(x_prompt, modp, jnp.zeros((bp, SUBLANES, 3 * D_HEADS), F32), 1, tm_p)
    qb_p, k_p, v_p, kb_p, vb_p, small_p, cum_p = pf
    gq_p, gk_p, gv_p, gz_p, cst_p = pg
    f_t = cum_p[..., :N_HEADS].transpose(0, 2, 1).reshape(bp * N_HEADS, s)
    o_fox_p = fox_prompt_attention(qb_p, kb_p, vb_p, f_t.reshape(bp * N_HEADS, s, 1),
                                   f_t.reshape(bp * N_HEADS, 1, s), tq=512)
    o_gdn_p, s_p = gdn_chunks(gq_p, gk_p, gv_p, gz_p, small_p, cum_p,
                              jnp.zeros((bp, N_HEADS, HEAD_DIM, HEAD_DIM), F32), nw)
    x1_p, h2_p, lg_p = out_proj(x_prompt, o_fox_p, o_gdn_p, modp[:, :, 2], modp[:, :, 4], modp[:, :, 3],
                                rw2, w_top, w_bot, w_router_p, b_router_p, group=1, tm=tm_p)

    conv_in = jnp.pad(state_conv[l], ((0, 0), (SUBLANES - (CONV_W - 1), 0), (0, 0)))
    sf, sg = mixer_inputs(x_sample, mods, conv_in, bd, t)
    qb_s, k_s, v_s, kb_s, vb_s, small_s, cum_s = sf
    gq_s, gk_s, gv_s, gz_s, cst_s = sg
    r_rows = logf_suffix(cache_logf[l], page_table).transpose(0, 2, 1)
    cs = cum_s[..., :N_HEADS]
    cs_rows = jnp.pad(cs.transpose(0, 2, 1), ((0, 0), (0, 0), (0, PAGE_SIZE - t)))
    cs_col = cs.transpose(0, 2, 1).reshape(bd, N_HEADS * t, 1)
    pad_rows = lambda a: jnp.pad(a, ((0, 0), (0, PAGE_SIZE - t), (0, 0)))
    o_fox_s = fox_sample_attention(qb_s, cache_k[l].reshape(n_pool, PAGE_SIZE, D_HEADS),
                                   cache_v[l].reshape(n_pool, PAGE_SIZE, D_HEADS), page_table, r_rows,
                                   pad_rows(kb_s), pad_rows(vb_s), cs_rows, cs_col, pp=4)
    padc = lambda a: jnp.pad(a, ((0, 0), (0, GDN_CHUNK - t), (0, 0)))
    cum_s_pad = jnp.pad(cum_s, ((0, 0), (0, GDN_CHUNK - t), (0, 0)), mode="edge")
    o_gdn_s, s_s = gdn_chunks(padc(gq_s), padc(gk_s), padc(gv_s), padc(gz_s), padc(small_s), cum_s_pad,
                              state_gdn[l].astype(F32), nw)
    x1_s, h2_s, lg_s = out_proj(x_sample, o_fox_s, o_gdn_s[:, :t], mods[:, :, 2], mods[:, :, 4], mods[:, :, 3],
                                rw2, w_top, w_bot, w_router_p, b_router_p, group=bd, tm=t)

    n_p, n_s = bp * s, bd * t
    h2_all = jnp.concatenate([h2_p.reshape(n_p, d), h2_s.reshape(n_s, d)], axis=0)
    logits = jnp.concatenate([lg_p.reshape(n_p, LANES), lg_s.reshape(n_s, LANES)], axis=0)[:, :N_EXPERTS]
    row_tok, row_gate, block_e, n_used, pos = _route(logits)
    x_sorted = h2_all[row_tok]
    hid = moe_gate_up(x_sorted, block_e, n_used, w_gu[l], b_gu[l].reshape(N_EXPERTS, 1, -1), tn=512)
    outg = moe_down(hid, block_e, n_used, w_down[l], b_down[l].reshape(N_EXPERTS, 1, -1),
                    row_gate.reshape(-1, 1), tn=1024)
    moe = outg[pos[:, 0]] + outg[pos[:, 1]] + outg[pos[:, 2]] + outg[pos[:, 3]]
    y_p = x1_p + modp[:, :, 5] * moe[:n_p].reshape(bp, s, d)
    y_s = x1_s + mods[:, :, 5] * moe[n_p:].reshape(bd, t, d)

    hd = lambda a, b_, t_: a.reshape(1, b_, t_, N_HEADS, HEAD_DIM)
    return (y_p, y_s,
            hd(k_p, bp, s), hd(v_p, bp, s), small_p[..., :N_HEADS][None],
            s_p[None], cst_p[:, SUBLANES - (CONV_W - 1):][None],
            hd(k_s, bd, t), hd(v_s, bd, t), small_s[..., :N_HEADS][None],
            s_s[None], cst_s[:, SUBLANES - (CONV_W - 1):][None])


def kernel(x_prompt, x_sample, cache_k, cache_v, cache_logf, state_gdn, state_conv, page_table, c_prompt, c_sample, w_ada, b_ada, rms_mix, rms_ffn, w_in, b_forget, q_norm, k_norm, conv_w, a_log, dt_bias, gdn_norm, w_out, w_router, b_router, w_gu, b_gu, w_down, b_down):
    return _forward(x_prompt, x_sample, cache_k, cache_v, cache_logf, state_gdn, state_conv, page_table,
                    c_prompt, c_sample, w_ada, b_ada, rms_mix, rms_ffn, w_in, b_forget, q_norm, k_norm,
                    conv_w, a_log, dt_bias, gdn_norm, w_out, w_router, b_router, w_gu, b_gu, w_down, b_down)
```

```python
import functools

import jax
import jax.numpy as jnp
from jax import lax
from jax.experimental import pallas as pl
from jax.experimental.pallas import tpu as pltpu

F32 = jnp.float32
BF16 = jnp.bfloat16

D_MODEL = 2048
HEAD_DIM = 128
N_HEADS = 8
D_HEADS = N_HEADS * HEAD_DIM
CONV_W = 4
PAGE_SIZE = 128
N_EXPERTS = 32
TOP_K = 4
N_ADA = 6
RMS_EPS = 1e-6
L2_EPS = 1e-6
SWIGLU_ALPHA = 1.702
SWIGLU_LIMIT = 7.0

LANES = 128
SUBLANES = 8
GDN_CHUNK = 128
MOE_ROWS = 256
NEG_BIG = -1e30

_ARB = "arbitrary"


def _cparams(n_axes, vmem_mb):
    return pltpu.CompilerParams(dimension_semantics=(_ARB,) * n_axes,
                                vmem_limit_bytes=vmem_mb * 1024 * 1024)


def _dot(a, b):
    return jnp.dot(a, b, preferred_element_type=F32)


def _dot_nt(a, b):
    return lax.dot_general(a, b, (((1,), (1,)), ((), ())), preferred_element_type=F32)


def _dot_tn(a, b):
    return lax.dot_general(a, b, (((0,), (0,)), ((), ())), preferred_element_type=F32)


def _bmm(a, b):
    return lax.dot_general(a, b, (((2,), (1,)), ((0,), (0,))), preferred_element_type=F32)


def _bmm_nt(a, b):
    return lax.dot_general(a, b, (((2,), (2,)), ((0,), (0,))), preferred_element_type=F32)


def _softplus(z):
    return jnp.maximum(z, 0.0) + jnp.log(1.0 + jnp.exp(-jnp.abs(z)))


def _sigmoid(z):
    return 1.0 / (1.0 + jnp.exp(-z))


def _ada_kernel(c_ref, w_ref, b_ref, o_ref):
    c = c_ref[...]
    a = (c * _sigmoid(c)).astype(BF16)
    o_ref[...] = _dot(a, w_ref[...].astype(BF16)) + b_ref[...]


def ada_mod(c_all, w_ada, b_ada):
    m, d = c_all.shape
    n = w_ada.shape[1]
    tn = min(1024, n)
    return pl.pallas_call(
        _ada_kernel,
        grid=(n // tn,),
        in_specs=[pl.BlockSpec((m, d), lambda j: (0, 0)),
                  pl.BlockSpec((d, tn), lambda j: (0, j)),
                  pl.BlockSpec((1, tn), lambda j: (0, j))],
        out_specs=pl.BlockSpec((m, tn), lambda j: (0, j)),
        out_shape=jax.ShapeDtypeStruct((m, n), F32),
        compiler_params=_cparams(1, 40),
        name="ada_mod",
    )(c_all, w_ada, b_ada.reshape(1, n))


def _mod_norm_bf16(x_ref, sc_ref, sh_ref, rw_ref):
    x = x_ref[...]
    g, tm, d = x.shape
    ms = jnp.mean(x * x, axis=-1, keepdims=True)
    h = x * lax.rsqrt(ms + RMS_EPS) * rw_ref[...]
    h = h * (1.0 + sc_ref[...]) + sh_ref[...]
    return h.reshape(g * tm, d).astype(BF16)


def _per_head(p, fn):
    return jnp.concatenate([fn(p[:, h * HEAD_DIM:(h + 1) * HEAD_DIM]) for h in range(N_HEADS)], axis=-1)


def _rms_head(s):
    return s * lax.rsqrt(jnp.mean(s * s, axis=-1, keepdims=True) + RMS_EPS)


def _l2_head(s):
    return s * lax.rsqrt(jnp.sum(s * s, axis=-1, keepdims=True) + L2_EPS)


def _proj_fox_kernel(x_ref, sc_ref, sh_ref, rw_ref, w_ref, qn_ref, kn_ref, pv_ref,
                     q_ref, k_ref, v_ref, kb_ref, vb_ref, small_ref, cum_ref, carry_ref, *, seg):
    i = pl.program_id(1)
    g, tm, _ = x_ref.shape
    m = g * tm
    hb = _mod_norm_bf16(x_ref, sc_ref, sh_ref, rw_ref)

    qn = _per_head(_dot(hb, w_ref[:, 0:D_HEADS]), _rms_head) * qn_ref[...]
    q_ref[...] = (qn * HEAD_DIM ** -0.5).astype(BF16).reshape(g, tm, D_HEADS)
    kn = _per_head(_dot(hb, w_ref[:, D_HEADS:2 * D_HEADS]), _rms_head) * kn_ref[...]
    k_ref[...] = kn.reshape(g, tm, D_HEADS)
    kb_ref[...] = kn.astype(BF16).reshape(g, tm, D_HEADS)
    vv = _dot(hb, w_ref[:, 2 * D_HEADS:3 * D_HEADS])
    v_ref[...] = vv.reshape(g, tm, D_HEADS)
    vb_ref[...] = vv.astype(BF16).reshape(g, tm, D_HEADS)

    z = _dot(hb, w_ref[:, 3 * D_HEADS:3 * D_HEADS + LANES]) + pv_ref[0:1, :]
    lane = lax.broadcasted_iota(jnp.int32, (m, LANES), 1)
    sp = _softplus(z)
    logsig = z - sp
    gval = -jnp.exp(pv_ref[1:2, :]) * sp
    is_f = lane < N_HEADS
    is_g = (lane >= N_HEADS) & (lane < 2 * N_HEADS)
    is_b = (lane >= 2 * N_HEADS) & (lane < 3 * N_HEADS)
    small = jnp.where(is_f, logsig, jnp.where(is_g, gval, jnp.where(is_b, _sigmoid(z), 0.0)))
    small_ref[...] = small.reshape(g, tm, LANES)

    row = lax.broadcasted_iota(jnp.int32, (m, LANES), 0)
    pos = row & (tm - 1)
    rowmod = jnp.where(is_g, pos & (seg - 1), pos)
    c = small
    s = 1
    while s < tm:
        c = c + jnp.where(rowmod >= s, pltpu.roll(c, s, axis=0), 0.0)
        s *= 2
    lane3 = lax.broadcasted_iota(jnp.int32, (g, tm, LANES), 2)
    carry = jnp.where(i == 0, 0.0, carry_ref[...])
    c3 = c.reshape(g, tm, LANES) + jnp.where(lane3 < N_HEADS, carry, 0.0)
    cum_ref[...] = c3
    carry_ref[...] = c3[:, tm - 1:tm, :]


def proj_fox(x3, sc, sh, rms_w, w_fox, qn_t, kn_t, pvec, *, group, tm):
    nseq, t, d = x3.shape
    g = group
    seg = min(GDN_CHUNK, tm)
    grid = (nseq // g, t // tm)
    blk = lambda w: pl.BlockSpec((g, tm, w), lambda s, i: (s, i, 0))
    const2 = lambda a: pl.BlockSpec(a.shape, lambda s, i: (0, 0))
    mod = pl.BlockSpec((g, 1, d), lambda s, i: (s, 0, 0))
    outs = [jax.ShapeDtypeStruct((nseq, t, D_HEADS), BF16),
            jax.ShapeDtypeStruct((nseq, t, D_HEADS), F32),
            jax.ShapeDtypeStruct((nseq, t, D_HEADS), F32),
            jax.ShapeDtypeStruct((nseq, t, D_HEADS), BF16),
            jax.ShapeDtypeStruct((nseq, t, D_HEADS), BF16),
            jax.ShapeDtypeStruct((nseq, t, LANES), F32),
            jax.ShapeDtypeStruct((nseq, t, LANES), F32)]
    return pl.pallas_call(
        functools.partial(_proj_fox_kernel, seg=seg),
        grid=grid,
        in_specs=[blk(d), mod, mod, const2(rms_w),
                  pl.BlockSpec(w_fox.shape, lambda s, i: (0, 0), pipeline_mode=pl.Buffered(1)),
                  const2(qn_t), const2(kn_t), const2(pvec)],
        out_specs=[blk(D_HEADS)] * 5 + [blk(LANES)] * 2,
        out_shape=outs,
        scratch_shapes=[pltpu.VMEM((g, 1, LANES), F32)],
        compiler_params=_cparams(2, 48),
        name="proj_fox",
    )(x3, sc, sh, rms_w, w_fox, qn_t, kn_t, pvec)


def _proj_gdn_kernel(x_ref, sc_ref, sh_ref, rw_ref, w_ref, cw_ref, cin_ref,
                     q_ref, k_ref, v_ref, z_ref, cst_ref, cbuf_ref):
    i = pl.program_id(1)
    g, tm, _ = x_ref.shape
    hb = _mod_norm_bf16(x_ref, sc_ref, sh_ref, rw_ref)
    outs = (q_ref, k_ref, v_ref)
    for c in range(3):
        cols = slice(c * D_HEADS, (c + 1) * D_HEADS)
        pc = _dot(hb, w_ref[:, cols]).reshape(g, tm, D_HEADS)

        @pl.when(i == 0)
        def _():
            cbuf_ref[c, :, 0:SUBLANES, :] = cin_ref[:, :, cols]

        @pl.when(i > 0)
        def _():
            cbuf_ref[c, :, 0:SUBLANES, :] = cbuf_ref[c, :, tm:tm + SUBLANES, :]

        cbuf_ref[c, :, SUBLANES:tm + SUBLANES, :] = pc
        cst_ref[:, :, cols] = pc[:, tm - SUBLANES:tm, :]
        base = SUBLANES - (CONV_W - 1)
        y = None
        for j in range(CONV_W):
            term = cbuf_ref[c, :, base + j:base + j + tm, :] * cw_ref[j:j + 1, cols]
            y = term if y is None else y + term
        y = (y * _sigmoid(y)).reshape(g * tm, D_HEADS)
        if c == 0:
            y = _per_head(y, _l2_head) * HEAD_DIM ** -0.5
        elif c == 1:
            y = _per_head(y, _l2_head)
        outs[c][...] = y.reshape(g, tm, D_HEADS)
    z_ref[...] = _dot(hb, w_ref[:, 3 * D_HEADS:4 * D_HEADS]).reshape(g, tm, D_HEADS)


def proj_gdn(x3, sc, sh, rms_w, w_g, conv_w, conv_in, *, group, tm):
    nseq, t, d = x3.shape
    g = group
    grid = (nseq // g, t // tm)
    blk = lambda w: pl.BlockSpec((g, tm, w), lambda s, i: (s, i, 0))
    const2 = lambda a: pl.BlockSpec(a.shape, lambda s, i: (0, 0))
    mod = pl.BlockSpec((g, 1, d), lambda s, i: (s, 0, 0))
    st = pl.BlockSpec((g, SUBLANES, 3 * D_HEADS), lambda s, i: (s, 0, 0))
    outs = [jax.ShapeDtypeStruct((nseq, t, D_HEADS), F32)] * 4 + \
           [jax.ShapeDtypeStruct((nseq, SUBLANES, 3 * D_HEADS), F32)]
    return pl.pallas_call(
        _proj_gdn_kernel,
        grid=grid,
        in_specs=[blk(d), mod, mod, const2(rms_w),
                  pl.BlockSpec(w_g.shape, lambda s, i: (0, 0), pipeline_mode=pl.Buffered(1)),
                  const2(conv_w), st],
        out_specs=[blk(D_HEADS)] * 4 + [st],
        out_shape=outs,
        scratch_shapes=[pltpu.VMEM((3, g, tm + SUBLANES, D_HEADS), F32)],
        compiler_params=_cparams(2, 52),
        name="proj_gdn",
    )(x3, sc, sh, rms_w, w_g, conv_w, conv_in)


def _fox_prompt_kernel(q_ref, k_ref, v_ref, fc_ref, fr_ref, o_ref, *, tq):
    qi = pl.program_id(2)
    q = q_ref[0]
    fq = fc_ref[0]

    def step(j, carry, masked):
        m_i, l_i, acc = carry
        off = pl.multiple_of(j * tq, tq)
        ks = k_ref[0, pl.ds(off, tq), :]
        vs = v_ref[0, pl.ds(off, tq), :]
        s = _dot_nt(q, ks) + (fq - fr_ref[0, :, pl.ds(off, tq)])
        if masked:
            r = lax.broadcasted_iota(jnp.int32, (tq, tq), 0)
            c = lax.broadcasted_iota(jnp.int32, (tq, tq), 1)
            s = jnp.where(c <= r, s, NEG_BIG)
        m_new = jnp.maximum(m_i, jnp.max(s, axis=-1, keepdims=True))
        alpha = jnp.exp(m_i - m_new)
        p = jnp.exp(s - m_new)
        l_new = alpha * l_i + jnp.sum(p, axis=-1, keepdims=True)
        acc = alpha * acc + _dot(p.astype(BF16), vs)
        return m_new, l_new, acc

    init = (jnp.full((tq, 1), NEG_BIG, F32), jnp.zeros((tq, 1), F32), jnp.zeros((tq, HEAD_DIM), F32))
    carry = lax.fori_loop(0, qi, functools.partial(step, masked=False), init)
    _, l_i, acc = step(qi, carry, True)
    o_ref[0] = (acc / l_i).astype(BF16)


def fox_prompt_attention(qb, kb, vb, f_col, f_row, *, tq):
    b, s, _ = qb.shape
    grid = (b, N_HEADS, s // tq)
    return pl.pallas_call(
        functools.partial(_fox_prompt_kernel, tq=tq),
        grid=grid,
        in_specs=[pl.BlockSpec((1, tq, HEAD_DIM), lambda bi, h, qi: (bi, qi, h)),
                  pl.BlockSpec((1, s, HEAD_DIM), lambda bi, h, qi: (bi, 0, h)),
                  pl.BlockSpec((1, s, HEAD_DIM), lambda bi, h, qi: (bi, 0, h)),
                  pl.BlockSpec((1, tq, 1), lambda bi, h, qi: (bi * N_HEADS + h, qi, 0)),
                  pl.BlockSpec((1, 1, s), lambda bi, h, qi: (bi * N_HEADS + h, 0, 0))],
        out_specs=pl.BlockSpec((1, tq, HEAD_DIM), lambda bi, h, qi: (bi, qi, h)),
        out_shape=jax.ShapeDtypeStruct((b, s, D_HEADS), BF16),
        compiler_params=_cparams(3, 32),
        name="fox_prompt_attention",
    )(qb, kb, vb, f_col, f_row)


def _fox_sample_kernel(pt_ref, q_ref, *refs, pp, t):
    k_refs = refs[0:pp]
    v_refs = refs[pp:2 * pp]
    lf_refs = refs[2 * pp:3 * pp]
    kn_ref, vn_ref, csn_ref, csq_ref, o_ref, m_ref, l_ref, acc_ref, carry_ref = refs[3 * pp:]
    s = pl.program_id(1)
    ns = pl.num_programs(1)
    hq = N_HEADS * t
    ncol = PAGE_SIZE * N_HEADS

    @pl.when(s == 0)
    def _():
        m_ref[...] = jnp.full(m_ref.shape, NEG_BIG, F32)
        l_ref[...] = jnp.zeros(l_ref.shape, F32)
        acc_ref[...] = jnp.zeros(acc_ref.shape, F32)
        carry_ref[...] = jnp.zeros(carry_ref.shape, F32)

    q = q_ref[0]
    csq = csq_ref[0]

    def update(sc, vals, width):
        m_old = m_ref[...]
        m_new = jnp.maximum(m_old, jnp.max(sc, axis=-1, keepdims=True))
        alpha = jnp.exp(m_old - m_new)
        p = jnp.exp(sc - m_new)
        l_ref[...] = alpha * l_ref[...] + jnp.sum(p, axis=-1, keepdims=True)
        pv = None
        for j, vv in enumerate(vals):
            term = _dot(p[:, j * width:(j + 1) * width].astype(BF16), vv)
            pv = term if pv is None else pv + term
        acc_ref[...] = alpha * acc_ref[...] + pv
        m_ref[...] = m_new

    lane = lax.broadcasted_iota(jnp.int32, (1, ncol), 1)
    carry = carry_ref[...]
    bias = [None] * pp
    for j in reversed(range(pp)):
        lf = lf_refs[j][...]
        c = lf
        step = N_HEADS
        while step < ncol:
            c = c + jnp.where(lane + step < ncol, pltpu.roll(c, ncol - step, axis=1), 0.0)
            step *= 2
        bias[j] = c - lf + carry
        tot = jnp.where(lane < N_HEADS, c, 0.0)
        step = N_HEADS
        while step < ncol:
            tot = tot + pltpu.roll(tot, step, axis=1)
            step *= 2
        carry = carry + tot
    carry_ref[...] = carry

    row = lax.broadcasted_iota(jnp.int32, (hq, ncol), 0)
    col = lax.broadcasted_iota(jnp.int32, (hq, ncol), 1)
    t_shift = t.bit_length() - 1
    h_shift = N_HEADS.bit_length() - 1
    same_head = (col & (N_HEADS - 1)) == (row >> t_shift)
    sc = [jnp.where(same_head, _dot_nt(q, k_refs[j][...].reshape(ncol, HEAD_DIM).astype(BF16)) + (csq + bias[j]),
                    NEG_BIG) for j in range(pp)]
    update(jnp.concatenate(sc, axis=1),
           [v_refs[j][...].reshape(ncol, HEAD_DIM).astype(BF16) for j in range(pp)], ncol)

    @pl.when(s == ns - 1)
    def _():
        rn = lax.broadcasted_iota(jnp.int32, (hq, LANES), 0)
        cn = lax.broadcasted_iota(jnp.int32, (hq, LANES), 1)
        ok = ((cn & (N_HEADS - 1)) == (rn >> t_shift)) & ((cn >> h_shift) <= (rn & (t - 1)))
        sn = jnp.where(ok, _dot_nt(q, kn_ref[0]) + (csq - csn_ref[0]), NEG_BIG)
        update(sn, [vn_ref[0]], LANES)
        o_ref[0] = (acc_ref[...] / l_ref[...]).astype(BF16)


def fox_sample_attention(q_rows, cache_k, cache_v, lf_rows, page_table, kn_pad, vn_pad, csn, csq, *, pp, t):
    bd, hq, _ = q_rows.shape
    n_pages = page_table.shape[1]
    ns = n_pages // pp
    ncol = PAGE_SIZE * N_HEADS
    page_of = lambda b, s, j, pt: pt[b, (ns - 1 - s) * pp + j]
    kv_spec = lambda j: pl.BlockSpec((None, None, PAGE_SIZE, N_HEADS, HEAD_DIM),
                                     lambda b, s, pt: (0, page_of(b, s, j, pt), 0, 0, 0))
    lf_spec = lambda j: pl.BlockSpec((None, 1, ncol), lambda b, s, pt: (page_of(b, s, j, pt), 0, 0))
    per_b = lambda shp: pl.BlockSpec((1,) + shp, lambda b, s, pt: (b, 0, 0))
    gs = pltpu.PrefetchScalarGridSpec(
        num_scalar_prefetch=1,
        grid=(bd, ns),
        in_specs=[per_b((hq, HEAD_DIM))] + [kv_spec(j) for j in range(pp)] * 2 + [lf_spec(j) for j in range(pp)] +
                 [per_b((LANES, HEAD_DIM)), per_b((LANES, HEAD_DIM)), per_b((1, LANES)), per_b((hq, 1))],
        out_specs=per_b((hq, HEAD_DIM)),
        scratch_shapes=[pltpu.VMEM((hq, 1), F32), pltpu.VMEM((hq, 1), F32), pltpu.VMEM((hq, HEAD_DIM), F32),
                        pltpu.VMEM((1, ncol), F32)])
    return pl.pallas_call(
        functools.partial(_fox_sample_kernel, pp=pp, t=t), grid_spec=gs,
        out_shape=jax.ShapeDtypeStruct((bd, hq, HEAD_DIM), BF16),
        compiler_params=_cparams(2, 48),
        name="fox_sample_attention",
    )(page_table, q_rows, *([cache_k] * pp), *([cache_v] * pp), *([lf_rows] * pp), kn_pad, vn_pad, csn, csq)


def _gdn_kernel(q_ref, k_ref, v_ref, z_ref, small_ref, cum_ref, s0_ref, nw_ref, o_ref, so_ref, s_ref):
    ci = pl.program_id(1)
    c = GDN_CHUNK

    @pl.when(ci == 0)
    def _():
        s_ref[...] = s0_ref[0]

    cum = cum_ref[0]
    cum_t = cum.T
    small = small_ref[0]
    r = lax.broadcasted_iota(jnp.int32, (c, c), 0)
    cc = lax.broadcasted_iota(jnp.int32, (c, c), 1)
    tri = cc <= r
    strict = cc < r
    eye = (cc == r).astype(F32)
    pair_masks = []
    shift = 0
    while (1 << shift) < c:
        rb = r >> shift
        pair_masks.append(((rb & 1) == 1) & ((cc >> shift) == rb - 1))
        shift += 1
    heads = range(N_HEADS)
    stack = lambda f: jnp.stack([f(h) for h in heads], axis=0)
    hs = lambda h: slice(h * HEAD_DIM, (h + 1) * HEAD_DIM)
    q = stack(lambda h: q_ref[0, :, hs(h)])
    k = stack(lambda h: k_ref[0, :, hs(h)])
    v = stack(lambda h: v_ref[0, :, hs(h)])
    gc = stack(lambda h: cum[:, N_HEADS + h:N_HEADS + h + 1])
    gr = stack(lambda h: cum_t[N_HEADS + h:N_HEADS + h + 1, :])
    bt = stack(lambda h: small[:, 2 * N_HEADS + h:2 * N_HEADS + h + 1])
    g_last = gc[:, c - 1:c, :]
    decay = jnp.where(tri, jnp.exp(jnp.where(tri, gc - gr, 0.0)), 0.0)
    kb = k * bt
    kk_qk = _bmm_nt(jnp.concatenate([kb, q], axis=1).astype(BF16), k.astype(BF16))
    lmat = jnp.where(strict, kk_qk[:, :c] * decay, 0.0)
    a_in = jnp.where(tri, kk_qk[:, c:] * decay, 0.0)
    tm = jnp.broadcast_to(eye, (N_HEADS, c, c))
    for e_mask in pair_masks:
        tb = tm.astype(BF16)
        te = _bmm(tb, jnp.where(e_mask, lmat, 0.0).astype(BF16))
        tm = tm - _bmm(te.astype(BF16), tb)
    eg = jnp.exp(gc)
    uwb = _bmm(tm.astype(BF16), jnp.concatenate([v * bt, kb * eg], axis=2).astype(BF16)).astype(BF16)
    au = _bmm(a_in.astype(BF16), uwb)
    kt = k * jnp.exp(g_last - gc)
    kt_t = stack(lambda h: kt[h].T)
    ku = _bmm(kt_t.astype(BF16), uwb)
    qt = q * eg - au[:, :, HEAD_DIM:]
    s_all = s_ref[...]
    x = _bmm(jnp.concatenate([-ku[:, :, HEAD_DIM:], qt], axis=1).astype(BF16), s_all.astype(BF16))
    s_new = s_all * jnp.exp(g_last) + x[:, :HEAD_DIM] + ku[:, :, :HEAD_DIM]
    s_ref[...] = s_new
    so_ref[0] = s_new
    o = x[:, HEAD_DIM:] + au[:, :, :HEAD_DIM]
    o = o * lax.rsqrt(jnp.mean(o * o, axis=-1, keepdims=True) + RMS_EPS) * nw_ref[...]
    for h in heads:
        zh = z_ref[0, :, hs(h)]
        o_ref[0, :, hs(h)] = (o[h] * (zh * _sigmoid(zh))).astype(BF16)


def gdn_chunks(q, k, v, z, small, cum, s0, norm_w):
    nseq, tp, _ = q.shape
    c = GDN_CHUNK
    blk = lambda w: pl.BlockSpec((1, c, w), lambda s, i: (s, i, 0))
    st = pl.BlockSpec((1, N_HEADS, HEAD_DIM, HEAD_DIM), lambda s, i: (s, 0, 0, 0))
    return pl.pallas_call(
        _gdn_kernel,
        grid=(nseq, tp // c),
        in_specs=[blk(D_HEADS)] * 4 + [blk(LANES)] * 2 + [st, pl.BlockSpec((1, HEAD_DIM), lambda s, i: (0, 0))],
        out_specs=[blk(D_HEADS), st],
        out_shape=[jax.ShapeDtypeStruct((nseq, tp, D_HEADS), BF16),
                   jax.ShapeDtypeStruct((nseq, N_HEADS, HEAD_DIM, HEAD_DIM), F32)],
        scratch_shapes=[pltpu.VMEM((N_HEADS, HEAD_DIM, HEAD_DIM), F32)],
        compiler_params=_cparams(2, 32),
        name="gdn_chunks",
    )(q, k, v, z, small, cum, s0, norm_w)


def _split3(a):
    hi = a.astype(BF16)
    lo = (a - hi.astype(F32)).astype(BF16)
    return hi, lo


def _out_proj_kernel(x_ref, of_ref, og_ref, g1_ref, sc_ref, sh_ref, rw_ref, wt_ref, wb_ref, wr_ref, br_ref,
                     x1_ref, h2_ref, lg_ref):
    g, tm, d = x_ref.shape
    m = g * tm
    mix = _dot(of_ref[...].reshape(m, D_HEADS), wt_ref[...]) + _dot(og_ref[...].reshape(m, D_HEADS), wb_ref[...])
    x1 = x_ref[...] + g1_ref[...] * mix.reshape(g, tm, d)
    x1_ref[...] = x1
    ms = jnp.mean(x1 * x1, axis=-1, keepdims=True)
    h2 = x1 * lax.rsqrt(ms + RMS_EPS) * rw_ref[...]
    h2 = (h2 * (1.0 + sc_ref[...]) + sh_ref[...]).reshape(m, d)
    h2_ref[...] = h2.reshape(g, tm, d)
    hh, hl = _split3(h2)
    wh, wl = _split3(wr_ref[...])
    lg = _dot(hh, wh) + (_dot(hh, wl) + _dot(hl, wh)) + br_ref[...]
    lg_ref[...] = lg.reshape(g, tm, LANES)


def out_proj(x3, o_fox, o_gdn, g1, sc2, sh2, rms_w, w_top, w_bot, w_router_p, b_router_p, *, group, tm):
    nseq, t, d = x3.shape
    g = group
    blk = lambda w: pl.BlockSpec((g, tm, w), lambda s, i: (s, i, 0))
    const2 = lambda a: pl.BlockSpec(a.shape, lambda s, i: (0, 0))
    res = lambda a: pl.BlockSpec(a.shape, lambda s, i: (0, 0), pipeline_mode=pl.Buffered(1))
    mod = pl.BlockSpec((g, 1, d), lambda s, i: (s, 0, 0))
    return pl.pallas_call(
        _out_proj_kernel,
        grid=(nseq // g, t // tm),
        in_specs=[blk(d), blk(D_HEADS), blk(D_HEADS), mod, mod, mod, const2(rms_w),
                  res(w_top), res(w_bot), const2(w_router_p), const2(b_router_p)],
        out_specs=[blk(d), blk(d), blk(LANES)],
        out_shape=[jax.ShapeDtypeStruct((nseq, t, d), F32), jax.ShapeDtypeStruct((nseq, t, d), F32),
                   jax.ShapeDtypeStruct((nseq, t, LANES), F32)],
        compiler_params=_cparams(2, 40),
        name="out_proj",
    )(x3, o_fox, o_gdn, g1, sc2, sh2, rms_w, w_top, w_bot, w_router_p, b_router_p)


def _row_copy(src_hbm, buf_ref, sem, tok, r):
    return pltpu.make_async_copy(src_hbm.at[pl.ds(tok, 1)], buf_ref.at[pl.ds(r, 1)], sem)


def _moe_dispatch_kernel(tok_ref, nu_ref, src_hbm, o_ref, buf_ref, sem):
    i = pl.program_id(0)
    rows = buf_ref.shape[0]

    @pl.when(i < nu_ref[0])
    def _():
        base = i * rows

        def issue(r, carry):
            _row_copy(src_hbm, buf_ref, sem, tok_ref[base + r], r).start()
            return carry

        def drain(r, carry):
            _row_copy(src_hbm, buf_ref, sem, 0, r).wait()
            return carry

        lax.fori_loop(0, rows, issue, 0, unroll=8)
        lax.fori_loop(0, rows, drain, 0, unroll=8)
        o_ref[...] = buf_ref[...].astype(BF16)

    @pl.when(i >= nu_ref[0])
    def _():
        o_ref[...] = jnp.zeros(o_ref.shape, o_ref.dtype)


def moe_dispatch(h2_all, row_tok, n_used):
    n, d = h2_all.shape
    r = row_tok.shape[0]
    gs = pltpu.PrefetchScalarGridSpec(
        num_scalar_prefetch=2,
        grid=(r // MOE_ROWS,),
        in_specs=[pl.BlockSpec(memory_space=pl.ANY)],
        out_specs=pl.BlockSpec((MOE_ROWS, d), lambda i, tok, nu: (i, 0)),
        scratch_shapes=[pltpu.VMEM((MOE_ROWS, d), F32), pltpu.SemaphoreType.DMA(())])
    return pl.pallas_call(
        _moe_dispatch_kernel, grid_spec=gs,
        out_shape=jax.ShapeDtypeStruct((r, d), BF16),
        compiler_params=_cparams(1, 16),
        name="moe_dispatch",
    )(row_tok, n_used, h2_all)


def _moe_gu_kernel(be_ref, nu_ref, x_ref, wg_ref, wl_ref, bg_ref, bl_ref, h_ref, wgb_ref, wlb_ref):
    i = pl.program_id(1)

    @pl.when(i < nu_ref[0])
    def _():
        @pl.when((i == 0) | (be_ref[i] != be_ref[jnp.maximum(i - 1, 0)]))
        def _():
            wgb_ref[...] = wg_ref[0].astype(BF16)
            wlb_ref[...] = wl_ref[0].astype(BF16)

        x = x_ref[...]
        glu = jnp.minimum(_dot(x, wgb_ref[...]) + bg_ref[0], SWIGLU_LIMIT)
        lin = jnp.clip(_dot(x, wlb_ref[...]) + bl_ref[0], -SWIGLU_LIMIT, SWIGLU_LIMIT)
        h_ref[...] = (glu * _sigmoid(SWIGLU_ALPHA * glu) * (lin + 1.0)).astype(BF16)

    @pl.when(i >= nu_ref[0])
    def _():
        h_ref[...] = jnp.zeros(h_ref.shape, h_ref.dtype)


def moe_gate_up(x_sorted, block_e, n_used, w_gu, b_gu3, *, tn):
    r, d = x_sorted.shape
    ne, _, two_ff = w_gu.shape
    d_ff = two_ff // 2
    nj = d_ff // tn
    nb = r // MOE_ROWS
    blk_i = lambda i, nu: jnp.minimum(i, nu[0] - 1)
    gs = pltpu.PrefetchScalarGridSpec(
        num_scalar_prefetch=2,
        grid=(nj, nb),
        in_specs=[pl.BlockSpec((MOE_ROWS, d), lambda j, i, be, nu: (blk_i(i, nu), 0)),
                  pl.BlockSpec((1, d, tn), lambda j, i, be, nu: (be[blk_i(i, nu)], 0, j)),
                  pl.BlockSpec((1, d, tn), lambda j, i, be, nu: (be[blk_i(i, nu)], 0, nj + j)),
                  pl.BlockSpec((1, 1, tn), lambda j, i, be, nu: (be[blk_i(i, nu)], 0, j)),
                  pl.BlockSpec((1, 1, tn), lambda j, i, be, nu: (be[blk_i(i, nu)], 0, nj + j))],
        out_specs=pl.BlockSpec((MOE_ROWS, tn), lambda j, i, be, nu: (i, j)),
        scratch_shapes=[pltpu.VMEM((d, tn), BF16), pltpu.VMEM((d, tn), BF16)])
    return pl.pallas_call(
        _moe_gu_kernel, grid_spec=gs,
        out_shape=jax.ShapeDtypeStruct((r, d_ff), BF16),
        compiler_params=_cparams(2, 40),
        name="moe_gate_up",
    )(block_e, n_used, x_sorted, w_gu, w_gu, b_gu3, b_gu3)


def _moe_down_kernel(be_ref, nu_ref, h_ref, w_ref, b_ref, gate_ref, o_ref, wb_ref):
    i = pl.program_id(1)

    @pl.when(i < nu_ref[0])
    def _():
        @pl.when((i == 0) | (be_ref[i] != be_ref[jnp.maximum(i - 1, 0)]))
        def _():
            wb_ref[...] = w_ref[0].astype(BF16)

        o_ref[...] = (_dot(h_ref[...], wb_ref[...]) + b_ref[0]) * gate_ref[...]

    @pl.when(i >= nu_ref[0])
    def _():
        o_ref[...] = jnp.zeros(o_ref.shape, o_ref.dtype)


def moe_down(h_sorted, block_e, n_used, w_down, b_down3, row_gate, *, tn):
    r, d_ff = h_sorted.shape
    d = w_down.shape[2]
    nj = d // tn
    nb = r // MOE_ROWS
    blk_i = lambda i, nu: jnp.minimum(i, nu[0] - 1)
    gs = pltpu.PrefetchScalarGridSpec(
        num_scalar_prefetch=2,
        grid=(nj, nb),
        in_specs=[pl.BlockSpec((MOE_ROWS, d_ff), lambda j, i, be, nu: (blk_i(i, nu), 0)),
                  pl.BlockSpec((1, d_ff, tn), lambda j, i, be, nu: (be[blk_i(i, nu)], 0, j)),
                  pl.BlockSpec((1, 1, tn), lambda j, i, be, nu: (be[blk_i(i, nu)], 0, j)),
                  pl.BlockSpec((MOE_ROWS, 1), lambda j, i, be, nu: (blk_i(i, nu), 0))],
        out_specs=pl.BlockSpec((MOE_ROWS, tn), lambda j, i, be, nu: (i, j)),
        scratch_shapes=[pltpu.VMEM((d_ff, tn), BF16)])
    return pl.pallas_call(
        _moe_down_kernel, grid_spec=gs,
        out_shape=jax.ShapeDtypeStruct((r, d), F32),
        compiler_params=_cparams(2, 40),
        name="moe_down",
    )(block_e, n_used, h_sorted, w_down, b_down3, row_gate)


def _route(logits):
    n = logits.shape[0]
    a = n * TOP_K
    top_val, top_idx = lax.top_k(logits, TOP_K)
    gate = jax.nn.softmax(top_val, axis=-1)
    flat_e = top_idx.reshape(a)
    order = jnp.argsort(flat_e)
    sorted_e = flat_e[order]
    counts = jnp.bincount(flat_e, length=N_EXPERTS)
    starts = jnp.cumsum(counts) - counts
    padded = -(-counts // MOE_ROWS) * MOE_ROWS
    pad_ends = jnp.cumsum(padded)
    dest = (pad_ends - padded)[sorted_e] + jnp.arange(a) - starts[sorted_e]
    nb = -(-(a + N_EXPERTS * (MOE_ROWS - 1)) // MOE_ROWS)
    r = nb * MOE_ROWS
    row_tok = jnp.zeros((r,), jnp.int32).at[dest].set((order // TOP_K).astype(jnp.int32))
    row_gate = jnp.zeros((r,), F32).at[dest].set(gate.reshape(a)[order])
    block_e = jnp.minimum(jnp.searchsorted(pad_ends, jnp.arange(nb) * MOE_ROWS, side='right'),
                          N_EXPERTS - 1).astype(jnp.int32)
    n_used = (pad_ends[-1] // MOE_ROWS).astype(jnp.int32).reshape(1)
    pos = jnp.zeros((a,), jnp.int32).at[order].set(dest.astype(jnp.int32)).reshape(n, TOP_K)
    return row_tok, row_gate, block_e, n_used, pos


def _forward(x_prompt, x_sample, cache_k, cache_v, cache_logf, state_gdn, state_conv, page_table,
             c_prompt, c_sample, w_ada, b_ada, rms_mix, rms_ffn, w_in, b_forget, q_norm, k_norm,
             conv_w, a_log, dt_bias, gdn_norm, w_out, w_router, b_router, w_gu, b_gu, w_down, b_down):
    depth = w_ada.shape[0]
    assert depth == 1, "single-layer trunk"
    l = 0
    bp, s, d = x_prompt.shape
    bd, t, _ = x_sample.shape
    n_pool = cache_k.shape[1]
    tm_p = 256

    n_c = bp + bd
    c_all = jnp.concatenate([c_prompt, c_sample, jnp.zeros((-n_c % SUBLANES, d), F32)], axis=0)
    mod = ada_mod(c_all, w_ada[l], b_ada[l])
    modp = mod[:bp].reshape(bp, 1, N_ADA, d)
    mods = mod[bp:n_c].reshape(bd, 1, N_ADA, d)

    w = w_in[l]
    o_ff, o_g, o_ab = 3 * D_HEADS, 3 * D_HEADS + N_HEADS, 7 * D_HEADS + N_HEADS
    w_small = jnp.concatenate([w[:, o_ff:o_g], w[:, o_ab:o_ab + 2 * N_HEADS]], axis=1)
    w_small = jnp.pad(w_small, ((0, 0), (0, LANES - 3 * N_HEADS)))
    w_fox = jnp.concatenate([w[:, :o_ff], w_small], axis=1).astype(BF16)
    w_g = w[:, o_g:o_ab].astype(BF16)
    qn_t = jnp.tile(q_norm[l], N_HEADS).reshape(1, D_HEADS)
    kn_t = jnp.tile(k_norm[l], N_HEADS).reshape(1, D_HEADS)
    pvec = jnp.zeros((SUBLANES, LANES), F32)
    pvec = pvec.at[0, 0:N_HEADS].set(b_forget[l]).at[0, N_HEADS:2 * N_HEADS].set(dt_bias[l])
    pvec = pvec.at[1, N_HEADS:2 * N_HEADS].set(a_log[l])
    rw1 = rms_mix[l].reshape(1, d)
    rw2 = rms_ffn[l].reshape(1, d)
    w_top = w_out[l][:D_HEADS].astype(BF16)
    w_bot = w_out[l][D_HEADS:].astype(BF16)
    w_router_p = jnp.pad(w_router[l], ((0, 0), (0, LANES - N_EXPERTS)))
    b_router_p = jnp.pad(b_router[l], (0, LANES - N_EXPERTS)).reshape(1, LANES)
    nw = gdn_norm[l].reshape(1, HEAD_DIM)

    def mixer_inputs(x3, m4, conv_in, group, tm):
        sh1, sc1 = m4[:, :, 0], m4[:, :, 1]
        pf = proj_fox(x3, sc1, sh1, rw1, w_fox, qn_t, kn_t, pvec, group=group, tm=tm)
        pg = proj_gdn(x3, sc1, sh1, rw1, w_g, conv_w[l], conv_in, group=group, tm=tm)
        return pf, pg

    pf, pg = mixer_inputs(x_prompt, modp, jnp.zeros((bp, SUBLANES, 3 * D_HEADS), F32), 1, tm_p)
    qb_p, k_p, v_p, kb_p, vb_p, small_p, cum_p = pf
    gq_p, gk_p, gv_p, gz_p, cst_p = pg
    f_t = cum_p[..., :N_HEADS].transpose(0, 2, 1).reshape(bp * N_HEADS, s)
    o_fox_p = fox_prompt_attention(qb_p, kb_p, vb_p, f_t.reshape(bp * N_HEADS, s, 1),
                                   f_t.reshape(bp * N_HEADS, 1, s), tq=512)
    o_gdn_p, s_p = gdn_chunks(gq_p, gk_p, gv_p, gz_p, small_p, cum_p,
                              jnp.zeros((bp, N_HEADS, HEAD_DIM, HEAD_DIM), F32), nw)
    x1_p, h2_p, lg_p = out_proj(x_prompt, o_fox_p, o_gdn_p, modp[:, :, 2], modp[:, :, 4], modp[:, :, 3],
                                rw2, w_top, w_bot, w_router_p, b_router_p, group=1, tm=tm_p)

    conv_in = jnp.pad(state_conv[l], ((0, 0), (SUBLANES - (CONV_W - 1), 0), (0, 0)))
    sf, sg = mixer_inputs(x_sample, mods, conv_in, bd, t)
    qb_s, k_s, v_s, kb_s, vb_s, small_s, cum_s = sf
    gq_s, gk_s, gv_s, gz_s, cst_s = sg
    hq = N_HEADS * t
    cs = cum_s[..., :N_HEADS]
    csn = jnp.pad(cs.reshape(bd, 1, hq), ((0, 0), (0, 0), (0, LANES - hq)))
    csq = cs.transpose(0, 2, 1).reshape(bd, hq, 1)
    q_rows = qb_s.reshape(bd, t, N_HEADS, HEAD_DIM).transpose(0, 2, 1, 3).reshape(bd, hq, HEAD_DIM)
    new_rows = lambda a: jnp.pad(a.reshape(bd, hq, HEAD_DIM), ((0, 0), (0, LANES - hq), (0, 0)))
    lf_rows = cache_logf[l].reshape(n_pool, 1, PAGE_SIZE * N_HEADS)
    o_rows = fox_sample_attention(q_rows, cache_k, cache_v, lf_rows, page_table, new_rows(kb_s), new_rows(vb_s),
                                  csn, csq, pp=8, t=t)
    o_fox_s = o_rows.reshape(bd, N_HEADS, t, HEAD_DIM).transpose(0, 2, 1, 3).reshape(bd, t, D_HEADS)
    padc = lambda a: jnp.pad(a, ((0, 0), (0, GDN_CHUNK - t), (0, 0)))
    cum_s_pad = jnp.pad(cum_s, ((0, 0), (0, GDN_CHUNK - t), (0, 0)), mode="edge")
    o_gdn_s, s_s = gdn_chunks(padc(gq_s), padc(gk_s), padc(gv_s), padc(gz_s), padc(small_s), cum_s_pad,
                              state_gdn[l].astype(F32), nw)
    x1_s, h2_s, lg_s = out_proj(x_sample, o_fox_s, o_gdn_s[:, :t], mods[:, :, 2], mods[:, :, 4], mods[:, :, 3],
                                rw2, w_top, w_bot, w_router_p, b_router_p, group=bd, tm=t)

    n_p, n_s = bp * s, bd * t
    h2_all = jnp.concatenate([h2_p.reshape(n_p, d), h2_s.reshape(n_s, d)], axis=0)
    logits = jnp.concatenate([lg_p.reshape(n_p, LANES), lg_s.reshape(n_s, LANES)], axis=0)[:, :N_EXPERTS]
    row_tok, row_gate, block_e, n_used, pos = _route(logits)
    x_sorted = moe_dispatch(h2_all, row_tok, n_used)
    hid = moe_gate_up(x_sorted, block_e, n_used, w_gu[l], b_gu[l].reshape(N_EXPERTS, 1, -1), tn=512)
    outg = moe_down(hid, block_e, n_used, w_down[l], b_down[l].reshape(N_EXPERTS, 1, -1),
                    row_gate.reshape(-1, 1), tn=1024)
    moe = outg[pos[:, 0]] + outg[pos[:, 1]] + outg[pos[:, 2]] + outg[pos[:, 3]]
    y_p = x1_p + modp[:, :, 5] * moe[:n_p].reshape(bp, s, d)
    y_s = x1_s + mods[:, :, 5] * moe[n_p:].reshape(bd, t, d)

    hd = lambda a, b_, t_: a.reshape(1, b_, t_, N_HEADS, HEAD_DIM)
    return (y_p, y_s,
            hd(k_p, bp, s), hd(v_p, bp, s), small_p[..., :N_HEADS][None],
            s_p[None], cst_p[:, SUBLANES - (CONV_W - 1):][None],
            hd(k_s, bd, t), hd(v_s, bd, t), small_s[..., :N_HEADS][None],
            s_s[None], cst_s[:, SUBLANES - (CONV_W - 1):][None])


def kernel(x_prompt, x_sample, cache_k, cache_v, cache_logf, state_gdn, state_conv, page_table, c_prompt, c_sample, w_ada, b_ada, rms_mix, rms_ffn, w_in, b_forget, q_norm, k_norm, conv_w, a_log, dt_bias, gdn_norm, w_out, w_router, b_router, w_gu, b_gu, w_down, b_down):
    return _forward(x_prompt, x_sample, cache_k, cache_v, cache_logf, state_gdn, state_conv, page_table,
                    c_prompt, c_sample, w_ada, b_ada, rms_mix, rms_ffn, w_in, b_forget, q_norm, k_norm,
                    conv_w, a_log, dt_bias, gdn_norm, w_out, w_router, b_router, w_gu, b_gu, w_down, b_down)
```

```python
import functools

import jax
import jax.numpy as jnp
from jax import lax
from jax.experimental import pallas as pl
from jax.experimental.pallas import tpu as pltpu

F32 = jnp.float32
BF16 = jnp.bfloat16

D_MODEL = 2048
HEAD_DIM = 128
N_HEADS = 8
D_HEADS = N_HEADS * HEAD_DIM
CONV_W = 4
PAGE_SIZE = 128
N_EXPERTS = 32
TOP_K = 4
N_ADA = 6
RMS_EPS = 1e-6
L2_EPS = 1e-6
SWIGLU_ALPHA = 1.702
SWIGLU_LIMIT = 7.0

LANES = 128
SUBLANES = 8
GDN_CHUNK = 128
MOE_ROWS = 256
NEG_BIG = -1e30

_ARB = "arbitrary"


def _cparams(n_axes, vmem_mb):
    return pltpu.CompilerParams(dimension_semantics=(_ARB,) * n_axes,
                                vmem_limit_bytes=vmem_mb * 1024 * 1024)


def _dot(a, b):
    return jnp.dot(a, b, preferred_element_type=F32)


def _dot_nt(a, b):
    return lax.dot_general(a, b, (((1,), (1,)), ((), ())), preferred_element_type=F32)


def _dot_tn(a, b):
    return lax.dot_general(a, b, (((0,), (0,)), ((), ())), preferred_element_type=F32)


def _bmm(a, b):
    return lax.dot_general(a, b, (((2,), (1,)), ((0,), (0,))), preferred_element_type=F32)


def _bmm_nt(a, b):
    return lax.dot_general(a, b, (((2,), (2,)), ((0,), (0,))), preferred_element_type=F32)


def _softplus(z):
    return jnp.maximum(z, 0.0) + jnp.log(1.0 + jnp.exp(-jnp.abs(z)))


def _sigmoid(z):
    return 1.0 / (1.0 + jnp.exp(-z))


def _ada_kernel(c_ref, w_ref, b_ref, o_ref):
    c = c_ref[...]
    a = (c * _sigmoid(c)).astype(BF16)
    o_ref[...] = _dot(a, w_ref[...].astype(BF16)) + b_ref[...]


def ada_mod(c_all, w_ada, b_ada):
    m, d = c_all.shape
    n = w_ada.shape[1]
    tn = min(1024, n)
    return pl.pallas_call(
        _ada_kernel,
        grid=(n // tn,),
        in_specs=[pl.BlockSpec((m, d), lambda j: (0, 0)),
                  pl.BlockSpec((d, tn), lambda j: (0, j)),
                  pl.BlockSpec((1, tn), lambda j: (0, j))],
        out_specs=pl.BlockSpec((m, tn), lambda j: (0, j)),
        out_shape=jax.ShapeDtypeStruct((m, n), F32),
        compiler_params=_cparams(1, 40),
        name="ada_mod",
    )(c_all, w_ada, b_ada.reshape(1, n))


def _mod_norm_bf16(x_ref, sc_ref, sh_ref, rw_ref):
    x = x_ref[...]
    g, tm, d = x.shape
    ms = jnp.mean(x * x, axis=-1, keepdims=True)
    h = x * lax.rsqrt(ms + RMS_EPS) * rw_ref[...]
    h = h * (1.0 + sc_ref[...]) + sh_ref[...]
    return h.reshape(g * tm, d).astype(BF16)


def _per_head(p, fn):
    return jnp.concatenate([fn(p[:, h * HEAD_DIM:(h + 1) * HEAD_DIM]) for h in range(N_HEADS)], axis=-1)


def _rms_head(s):
    return s * lax.rsqrt(jnp.mean(s * s, axis=-1, keepdims=True) + RMS_EPS)


def _l2_head(s):
    return s * lax.rsqrt(jnp.sum(s * s, axis=-1, keepdims=True) + L2_EPS)


def _proj_fox_kernel(x_ref, sc_ref, sh_ref, rw_ref, w_ref, qn_ref, kn_ref, pv_ref,
                     q_ref, k_ref, v_ref, kb_ref, vb_ref, small_ref, cum_ref, carry_ref, *, seg):
    i = pl.program_id(1)
    g, tm, _ = x_ref.shape
    m = g * tm
    hb = _mod_norm_bf16(x_ref, sc_ref, sh_ref, rw_ref)

    qn = _per_head(_dot(hb, w_ref[:, 0:D_HEADS]), _rms_head) * qn_ref[...]
    q_ref[...] = (qn * HEAD_DIM ** -0.5).astype(BF16).reshape(g, tm, D_HEADS)
    kn = _per_head(_dot(hb, w_ref[:, D_HEADS:2 * D_HEADS]), _rms_head) * kn_ref[...]
    k_ref[...] = kn.reshape(g, tm, D_HEADS)
    kb_ref[...] = kn.astype(BF16).reshape(g, tm, D_HEADS)
    vv = _dot(hb, w_ref[:, 2 * D_HEADS:3 * D_HEADS])
    v_ref[...] = vv.reshape(g, tm, D_HEADS)
    vb_ref[...] = vv.astype(BF16).reshape(g, tm, D_HEADS)

    z = _dot(hb, w_ref[:, 3 * D_HEADS:3 * D_HEADS + LANES]) + pv_ref[0:1, :]
    lane = lax.broadcasted_iota(jnp.int32, (m, LANES), 1)
    sp = _softplus(z)
    logsig = z - sp
    gval = -jnp.exp(pv_ref[1:2, :]) * sp
    is_f = lane < N_HEADS
    is_g = (lane >= N_HEADS) & (lane < 2 * N_HEADS)
    is_b = (lane >= 2 * N_HEADS) & (lane < 3 * N_HEADS)
    small = jnp.where(is_f, logsig, jnp.where(is_g, gval, jnp.where(is_b, _sigmoid(z), 0.0)))
    small_ref[...] = small.reshape(g, tm, LANES)

    row = lax.broadcasted_iota(jnp.int32, (m, LANES), 0)
    pos = row & (tm - 1)
    rowmod = jnp.where(is_g, pos & (seg - 1), pos)
    c = small
    s = 1
    while s < tm:
        c = c + jnp.where(rowmod >= s, pltpu.roll(c, s, axis=0), 0.0)
        s *= 2
    lane3 = lax.broadcasted_iota(jnp.int32, (g, tm, LANES), 2)
    carry = jnp.where(i == 0, 0.0, carry_ref[...])
    c3 = c.reshape(g, tm, LANES) + jnp.where(lane3 < N_HEADS, carry, 0.0)
    cum_ref[...] = c3
    carry_ref[...] = c3[:, tm - 1:tm, :]


def proj_fox(x3, sc, sh, rms_w, w_fox, qn_t, kn_t, pvec, *, group, tm):
    nseq, t, d = x3.shape
    g = group
    seg = min(GDN_CHUNK, tm)
    grid = (nseq // g, t // tm)
    blk = lambda w: pl.BlockSpec((g, tm, w), lambda s, i: (s, i, 0))
    const2 = lambda a: pl.BlockSpec(a.shape, lambda s, i: (0, 0))
    mod = pl.BlockSpec((g, 1, d), lambda s, i: (s, 0, 0))
    outs = [jax.ShapeDtypeStruct((nseq, t, D_HEADS), BF16),
            jax.ShapeDtypeStruct((nseq, t, D_HEADS), F32),
            jax.ShapeDtypeStruct((nseq, t, D_HEADS), F32),
            jax.ShapeDtypeStruct((nseq, t, D_HEADS), BF16),
            jax.ShapeDtypeStruct((nseq, t, D_HEADS), BF16),
            jax.ShapeDtypeStruct((nseq, t, LANES), F32),
            jax.ShapeDtypeStruct((nseq, t, LANES), F32)]
    return pl.pallas_call(
        functools.partial(_proj_fox_kernel, seg=seg),
        grid=grid,
        in_specs=[blk(d), mod, mod, const2(rms_w),
                  pl.BlockSpec(w_fox.shape, lambda s, i: (0, 0), pipeline_mode=pl.Buffered(1)),
                  const2(qn_t), const2(kn_t), const2(pvec)],
        out_specs=[blk(D_HEADS)] * 5 + [blk(LANES)] * 2,
        out_shape=outs,
        scratch_shapes=[pltpu.VMEM((g, 1, LANES), F32)],
        compiler_params=_cparams(2, 48),
        name="proj_fox",
    )(x3, sc, sh, rms_w, w_fox, qn_t, kn_t, pvec)


def _proj_gdn_kernel(x_ref, sc_ref, sh_ref, rw_ref, w_ref, cw_ref, cin_ref,
                     q_ref, k_ref, v_ref, z_ref, cst_ref, cbuf_ref):
    i = pl.program_id(1)
    g, tm, _ = x_ref.shape
    hb = _mod_norm_bf16(x_ref, sc_ref, sh_ref, rw_ref)
    outs = (q_ref, k_ref, v_ref)
    for c in range(3):
        cols = slice(c * D_HEADS, (c + 1) * D_HEADS)
        pc = _dot(hb, w_ref[:, cols]).reshape(g, tm, D_HEADS)

        @pl.when(i == 0)
        def _():
            cbuf_ref[c, :, 0:SUBLANES, :] = cin_ref[:, :, cols]

        @pl.when(i > 0)
        def _():
            cbuf_ref[c, :, 0:SUBLANES, :] = cbuf_ref[c, :, tm:tm + SUBLANES, :]

        cbuf_ref[c, :, SUBLANES:tm + SUBLANES, :] = pc
        cst_ref[:, :, cols] = pc[:, tm - SUBLANES:tm, :]
        base = SUBLANES - (CONV_W - 1)
        y = None
        for j in range(CONV_W):
            term = cbuf_ref[c, :, base + j:base + j + tm, :] * cw_ref[j:j + 1, cols]
            y = term if y is None else y + term
        y = (y * _sigmoid(y)).reshape(g * tm, D_HEADS)
        if c == 0:
            y = _per_head(y, _l2_head) * HEAD_DIM ** -0.5
        elif c == 1:
            y = _per_head(y, _l2_head)
        outs[c][...] = y.reshape(g, tm, D_HEADS)
    z_ref[...] = _dot(hb, w_ref[:, 3 * D_HEADS:4 * D_HEADS]).reshape(g, tm, D_HEADS)


def proj_gdn(x3, sc, sh, rms_w, w_g, conv_w, conv_in, *, group, tm):
    nseq, t, d = x3.shape
    g = group
    grid = (nseq // g, t // tm)
    blk = lambda w: pl.BlockSpec((g, tm, w), lambda s, i: (s, i, 0))
    const2 = lambda a: pl.BlockSpec(a.shape, lambda s, i: (0, 0))
    mod = pl.BlockSpec((g, 1, d), lambda s, i: (s, 0, 0))
    st = pl.BlockSpec((g, SUBLANES, 3 * D_HEADS), lambda s, i: (s, 0, 0))
    outs = [jax.ShapeDtypeStruct((nseq, t, D_HEADS), F32)] * 4 + \
           [jax.ShapeDtypeStruct((nseq, SUBLANES, 3 * D_HEADS), F32)]
    return pl.pallas_call(
        _proj_gdn_kernel,
        grid=grid,
        in_specs=[blk(d), mod, mod, const2(rms_w),
                  pl.BlockSpec(w_g.shape, lambda s, i: (0, 0), pipeline_mode=pl.Buffered(1)),
                  const2(conv_w), st],
        out_specs=[blk(D_HEADS)] * 4 + [st],
        out_shape=outs,
        scratch_shapes=[pltpu.VMEM((3, g, tm + SUBLANES, D_HEADS), F32)],
        compiler_params=_cparams(2, 52),
        name="proj_gdn",
    )(x3, sc, sh, rms_w, w_g, conv_w, conv_in)


def _fox_prompt_kernel(q_ref, k_ref, v_ref, fc_ref, fr_ref, o_ref, *, tq):
    qi = pl.program_id(2)
    q = q_ref[0]
    fq = fc_ref[0]

    def step(j, carry, masked):
        m_i, l_i, acc = carry
        off = pl.multiple_of(j * tq, tq)
        ks = k_ref[0, pl.ds(off, tq), :]
        vs = v_ref[0, pl.ds(off, tq), :]
        s = _dot_nt(q, ks) + (fq - fr_ref[0, :, pl.ds(off, tq)])
        if masked:
            r = lax.broadcasted_iota(jnp.int32, (tq, tq), 0)
            c = lax.broadcasted_iota(jnp.int32, (tq, tq), 1)
            s = jnp.where(c <= r, s, NEG_BIG)
        m_new = jnp.maximum(m_i, jnp.max(s, axis=-1, keepdims=True))
        alpha = jnp.exp(m_i - m_new)
        p = jnp.exp(s - m_new)
        l_new = alpha * l_i + jnp.sum(p, axis=-1, keepdims=True)
        acc = alpha * acc + _dot(p.astype(BF16), vs)
        return m_new, l_new, acc

    init = (jnp.full((tq, 1), NEG_BIG, F32), jnp.zeros((tq, 1), F32), jnp.zeros((tq, HEAD_DIM), F32))
    carry = lax.fori_loop(0, qi, functools.partial(step, masked=False), init)
    _, l_i, acc = step(qi, carry, True)
    o_ref[0] = (acc / l_i).astype(BF16)


def fox_prompt_attention(qb, kb, vb, f_col, f_row, *, tq):
    b, s, _ = qb.shape
    grid = (b, N_HEADS, s // tq)
    return pl.pallas_call(
        functools.partial(_fox_prompt_kernel, tq=tq),
        grid=grid,
        in_specs=[pl.BlockSpec((1, tq, HEAD_DIM), lambda bi, h, qi: (bi, qi, h)),
                  pl.BlockSpec((1, s, HEAD_DIM), lambda bi, h, qi: (bi, 0, h)),
                  pl.BlockSpec((1, s, HEAD_DIM), lambda bi, h, qi: (bi, 0, h)),
                  pl.BlockSpec((1, tq, 1), lambda bi, h, qi: (bi * N_HEADS + h, qi, 0)),
                  pl.BlockSpec((1, 1, s), lambda bi, h, qi: (bi * N_HEADS + h, 0, 0))],
        out_specs=pl.BlockSpec((1, tq, HEAD_DIM), lambda bi, h, qi: (bi, qi, h)),
        out_shape=jax.ShapeDtypeStruct((b, s, D_HEADS), BF16),
        compiler_params=_cparams(3, 32),
        name="fox_prompt_attention",
    )(qb, kb, vb, f_col, f_row)


def _fox_sample_kernel(pt_ref, q_ref, *refs, pp, t):
    k_refs = refs[0:pp]
    v_refs = refs[pp:2 * pp]
    lf_refs = refs[2 * pp:3 * pp]
    kn_ref, vn_ref, csn_ref, csq_ref, o_ref, m_ref, l_ref, acc_ref, carry_ref = refs[3 * pp:]
    s = pl.program_id(1)
    ns = pl.num_programs(1)
    hq = N_HEADS * t
    ncol = PAGE_SIZE * N_HEADS

    @pl.when(s == 0)
    def _():
        m_ref[...] = jnp.full(m_ref.shape, NEG_BIG, F32)
        l_ref[...] = jnp.zeros(l_ref.shape, F32)
        acc_ref[...] = jnp.zeros(acc_ref.shape, F32)
        carry_ref[...] = jnp.zeros(carry_ref.shape, F32)

    q = q_ref[0]
    csq = csq_ref[0]

    def update(sc, vals, width):
        m_old = m_ref[...]
        m_new = jnp.maximum(m_old, jnp.max(sc, axis=-1, keepdims=True))
        alpha = jnp.exp(m_old - m_new)
        p = jnp.exp(sc - m_new)
        l_ref[...] = alpha * l_ref[...] + jnp.sum(p, axis=-1, keepdims=True)
        pv = None
        for j, vv in enumerate(vals):
            term = _dot(p[:, j * width:(j + 1) * width].astype(BF16), vv)
            pv = term if pv is None else pv + term
        acc_ref[...] = alpha * acc_ref[...] + pv
        m_ref[...] = m_new

    lane = lax.broadcasted_iota(jnp.int32, (1, ncol), 1)
    carry = carry_ref[...]
    bias = [None] * pp
    for j in reversed(range(pp)):
        lf = lf_refs[j][...]
        c = lf
        step = N_HEADS
        while step < ncol:
            c = c + jnp.where(lane + step < ncol, pltpu.roll(c, ncol - step, axis=1), 0.0)
            step *= 2
        bias[j] = c - lf + carry
        tot = jnp.where(lane < N_HEADS, c, 0.0)
        step = N_HEADS
        while step < ncol:
            tot = tot + pltpu.roll(tot, step, axis=1)
            step *= 2
        carry = carry + tot
    carry_ref[...] = carry

    row = lax.broadcasted_iota(jnp.int32, (hq, ncol), 0)
    col = lax.broadcasted_iota(jnp.int32, (hq, ncol), 1)
    t_shift = t.bit_length() - 1
    h_shift = N_HEADS.bit_length() - 1
    same_head = (col & (N_HEADS - 1)) == (row >> t_shift)
    sc = [jnp.where(same_head, _dot_nt(q, k_refs[j][...].reshape(ncol, HEAD_DIM).astype(BF16)) + (csq + bias[j]),
                    NEG_BIG) for j in range(pp)]
    update(jnp.concatenate(sc, axis=1),
           [v_refs[j][...].reshape(ncol, HEAD_DIM).astype(BF16) for j in range(pp)], ncol)

    @pl.when(s == ns - 1)
    def _():
        rn = lax.broadcasted_iota(jnp.int32, (hq, LANES), 0)
        cn = lax.broadcasted_iota(jnp.int32, (hq, LANES), 1)
        ok = ((cn & (N_HEADS - 1)) == (rn >> t_shift)) & ((cn >> h_shift) <= (rn & (t - 1)))
        sn = jnp.where(ok, _dot_nt(q, kn_ref[0]) + (csq - csn_ref[0]), NEG_BIG)
        update(sn, [vn_ref[0]], LANES)
        o_ref[0] = (acc_ref[...] / l_ref[...]).astype(BF16)


def fox_sample_attention(q_rows, cache_k, cache_v, lf_rows, page_table, kn_pad, vn_pad, csn, csq, *, pp, t):
    bd, hq, _ = q_rows.shape
    n_pages = page_table.shape[1]
    ns = n_pages // pp
    ncol = PAGE_SIZE * N_HEADS
    page_of = lambda b, s, j, pt: pt[b, (ns - 1 - s) * pp + j]
    kv_spec = lambda j: pl.BlockSpec((None, None, PAGE_SIZE, N_HEADS, HEAD_DIM),
                                     lambda b, s, pt: (0, page_of(b, s, j, pt), 0, 0, 0))
    lf_spec = lambda j: pl.BlockSpec((None, 1, ncol), lambda b, s, pt: (page_of(b, s, j, pt), 0, 0))
    per_b = lambda shp: pl.BlockSpec((1,) + shp, lambda b, s, pt: (b, 0, 0))
    gs = pltpu.PrefetchScalarGridSpec(
        num_scalar_prefetch=1,
        grid=(bd, ns),
        in_specs=[per_b((hq, HEAD_DIM))] + [kv_spec(j) for j in range(pp)] * 2 + [lf_spec(j) for j in range(pp)] +
                 [per_b((LANES, HEAD_DIM)), per_b((LANES, HEAD_DIM)), per_b((1, LANES)), per_b((hq, 1))],
        out_specs=per_b((hq, HEAD_DIM)),
        scratch_shapes=[pltpu.VMEM((hq, 1), F32), pltpu.VMEM((hq, 1), F32), pltpu.VMEM((hq, HEAD_DIM), F32),
                        pltpu.VMEM((1, ncol), F32)])
    return pl.pallas_call(
        functools.partial(_fox_sample_kernel, pp=pp, t=t), grid_spec=gs,
        out_shape=jax.ShapeDtypeStruct((bd, hq, HEAD_DIM), BF16),
        compiler_params=_cparams(2, 48),
        name="fox_sample_attention",
    )(page_table, q_rows, *([cache_k] * pp), *([cache_v] * pp), *([lf_rows] * pp), kn_pad, vn_pad, csn, csq)


def _gdn_kernel(q_ref, k_ref, v_ref, z_ref, small_ref, cum_ref, s0_ref, nw_ref, o_ref, so_ref, s_ref):
    ci = pl.program_id(1)
    c = GDN_CHUNK

    @pl.when(ci == 0)
    def _():
        s_ref[...] = s0_ref[0]

    cum = cum_ref[0]
    cum_t = cum.T
    small = small_ref[0]
    r = lax.broadcasted_iota(jnp.int32, (c, c), 0)
    cc = lax.broadcasted_iota(jnp.int32, (c, c), 1)
    tri = cc <= r
    strict = cc < r
    eye = (cc == r).astype(F32)
    pair_masks = []
    shift = 0
    while (1 << shift) < c:
        rb = r >> shift
        pair_masks.append(((rb & 1) == 1) & ((cc >> shift) == rb - 1))
        shift += 1
    heads = range(N_HEADS)
    stack = lambda f: jnp.stack([f(h) for h in heads], axis=0)
    hs = lambda h: slice(h * HEAD_DIM, (h + 1) * HEAD_DIM)
    q = stack(lambda h: q_ref[0, :, hs(h)])
    k = stack(lambda h: k_ref[0, :, hs(h)])
    v = stack(lambda h: v_ref[0, :, hs(h)])
    gc = stack(lambda h: cum[:, N_HEADS + h:N_HEADS + h + 1])
    gr = stack(lambda h: cum_t[N_HEADS + h:N_HEADS + h + 1, :])
    bt = stack(lambda h: small[:, 2 * N_HEADS + h:2 * N_HEADS + h + 1])
    g_last = gc[:, c - 1:c, :]
    decay = jnp.where(tri, jnp.exp(jnp.where(tri, gc - gr, 0.0)), 0.0)
    kb = k * bt
    kk_qk = _bmm_nt(jnp.concatenate([kb, q], axis=1).astype(BF16), k.astype(BF16))
    lmat = jnp.where(strict, kk_qk[:, :c] * decay, 0.0)
    a_in = jnp.where(tri, kk_qk[:, c:] * decay, 0.0)
    tm = jnp.broadcast_to(eye, (N_HEADS, c, c))
    for e_mask in pair_masks:
        tb = tm.astype(BF16)
        te = _bmm(tb, jnp.where(e_mask, lmat, 0.0).astype(BF16))
        tm = tm - _bmm(te.astype(BF16), tb)
    eg = jnp.exp(gc)
    uwb = _bmm(tm.astype(BF16), jnp.concatenate([v * bt, kb * eg], axis=2).astype(BF16)).astype(BF16)
    au = _bmm(a_in.astype(BF16), uwb)
    kt = k * jnp.exp(g_last - gc)
    kt_t = stack(lambda h: kt[h].T)
    ku = _bmm(kt_t.astype(BF16), uwb)
    qt = q * eg - au[:, :, HEAD_DIM:]
    s_all = s_ref[...]
    x = _bmm(jnp.concatenate([-ku[:, :, HEAD_DIM:], qt], axis=1).astype(BF16), s_all.astype(BF16))
    s_new = s_all * jnp.exp(g_last) + x[:, :HEAD_DIM] + ku[:, :, :HEAD_DIM]
    s_ref[...] = s_new
    so_ref[0] = s_new
    o = x[:, HEAD_DIM:] + au[:, :, :HEAD_DIM]
    o = o * lax.rsqrt(jnp.mean(o * o, axis=-1, keepdims=True) + RMS_EPS) * nw_ref[...]
    for h in heads:
        zh = z_ref[0, :, hs(h)]
        o_ref[0, :, hs(h)] = (o[h] * (zh * _sigmoid(zh))).astype(BF16)


def gdn_chunks(q, k, v, z, small, cum, s0, norm_w):
    nseq, tp, _ = q.shape
    c = GDN_CHUNK
    blk = lambda w: pl.BlockSpec((1, c, w), lambda s, i: (s, i, 0))
    st = pl.BlockSpec((1, N_HEADS, HEAD_DIM, HEAD_DIM), lambda s, i: (s, 0, 0, 0))
    return pl.pallas_call(
        _gdn_kernel,
        grid=(nseq, tp // c),
        in_specs=[blk(D_HEADS)] * 4 + [blk(LANES)] * 2 + [st, pl.BlockSpec((1, HEAD_DIM), lambda s, i: (0, 0))],
        out_specs=[blk(D_HEADS), st],
        out_shape=[jax.ShapeDtypeStruct((nseq, tp, D_HEADS), BF16),
                   jax.ShapeDtypeStruct((nseq, N_HEADS, HEAD_DIM, HEAD_DIM), F32)],
        scratch_shapes=[pltpu.VMEM((N_HEADS, HEAD_DIM, HEAD_DIM), F32)],
        compiler_params=_cparams(2, 32),
        name="gdn_chunks",
    )(q, k, v, z, small, cum, s0, norm_w)


def _split3(a):
    hi = a.astype(BF16)
    lo = (a - hi.astype(F32)).astype(BF16)
    return hi, lo


def _out_proj_kernel(x_ref, of_ref, og_ref, g1_ref, sc_ref, sh_ref, rw_ref, wt_ref, wb_ref, wr_ref, br_ref,
                     x1_ref, h2_ref, lg_ref):
    g, tm, d = x_ref.shape
    m = g * tm
    mix = _dot(of_ref[...].reshape(m, D_HEADS), wt_ref[...]) + _dot(og_ref[...].reshape(m, D_HEADS), wb_ref[...])
    x1 = x_ref[...] + g1_ref[...] * mix.reshape(g, tm, d)
    x1_ref[...] = x1
    ms = jnp.mean(x1 * x1, axis=-1, keepdims=True)
    h2 = x1 * lax.rsqrt(ms + RMS_EPS) * rw_ref[...]
    h2 = (h2 * (1.0 + sc_ref[...]) + sh_ref[...]).reshape(m, d)
    h2_ref[...] = h2.reshape(g, tm, d)
    hh, hl = _split3(h2)
    wh, wl = _split3(wr_ref[...])
    lg = _dot(hh, wh) + (_dot(hh, wl) + _dot(hl, wh)) + br_ref[...]
    lg_ref[...] = lg.reshape(g, tm, LANES)


def out_proj(x3, o_fox, o_gdn, g1, sc2, sh2, rms_w, w_top, w_bot, w_router_p, b_router_p, *, group, tm):
    nseq, t, d = x3.shape
    g = group
    blk = lambda w: pl.BlockSpec((g, tm, w), lambda s, i: (s, i, 0))
    const2 = lambda a: pl.BlockSpec(a.shape, lambda s, i: (0, 0))
    res = lambda a: pl.BlockSpec(a.shape, lambda s, i: (0, 0), pipeline_mode=pl.Buffered(1))
    mod = pl.BlockSpec((g, 1, d), lambda s, i: (s, 0, 0))
    return pl.pallas_call(
        _out_proj_kernel,
        grid=(nseq // g, t // tm),
        in_specs=[blk(d), blk(D_HEADS), blk(D_HEADS), mod, mod, mod, const2(rms_w),
                  res(w_top), res(w_bot), const2(w_router_p), const2(b_router_p)],
        out_specs=[blk(d), blk(d), blk(LANES)],
        out_shape=[jax.ShapeDtypeStruct((nseq, t, d), F32), jax.ShapeDtypeStruct((nseq, t, d), F32),
                   jax.ShapeDtypeStruct((nseq, t, LANES), F32)],
        compiler_params=_cparams(2, 40),
        name="out_proj",
    )(x3, o_fox, o_gdn, g1, sc2, sh2, rms_w, w_top, w_bot, w_router_p, b_router_p)


def _row_copy(src_hbm, buf_ref, sem, tok, r):
    return pltpu.make_async_copy(src_hbm.at[pl.ds(tok, 1)], buf_ref.at[pl.ds(r, 1)], sem)


def _moe_dispatch_kernel(tok_ref, nu_ref, src_hbm, o_ref, buf_ref, sem):
    i = pl.program_id(0)
    rows = buf_ref.shape[0]

    @pl.when(i < nu_ref[0])
    def _():
        base = i * rows

        def issue(r, carry):
            _row_copy(src_hbm, buf_ref, sem, tok_ref[base + r], r).start()
            return carry

        def drain(r, carry):
            _row_copy(src_hbm, buf_ref, sem, 0, r).wait()
            return carry

        lax.fori_loop(0, rows, issue, 0, unroll=8)
        lax.fori_loop(0, rows, drain, 0, unroll=8)
        o_ref[...] = buf_ref[...].astype(BF16)

    @pl.when(i >= nu_ref[0])
    def _():
        o_ref[...] = jnp.zeros(o_ref.shape, o_ref.dtype)


def moe_dispatch(h2_all, row_tok, n_used):
    n, d = h2_all.shape
    r = row_tok.shape[0]
    gs = pltpu.PrefetchScalarGridSpec(
        num_scalar_prefetch=2,
        grid=(r // MOE_ROWS,),
        in_specs=[pl.BlockSpec(memory_space=pl.ANY)],
        out_specs=pl.BlockSpec((MOE_ROWS, d), lambda i, tok, nu: (i, 0)),
        scratch_shapes=[pltpu.VMEM((MOE_ROWS, d), F32), pltpu.SemaphoreType.DMA(())])
    return pl.pallas_call(
        _moe_dispatch_kernel, grid_spec=gs,
        out_shape=jax.ShapeDtypeStruct((r, d), BF16),
        compiler_params=_cparams(1, 16),
        name="moe_dispatch",
    )(row_tok, n_used, h2_all)


def _expert_rows(e, n_experts, start_ref, cnt_ref, nu_ref, x_hbm, o_hbm, col0, xbuf, obuf, sem_in, sem_out, compute):
    rows = xbuf.shape[1]
    tn = obuf.shape[2]
    nb = cnt_ref[e]
    first = start_ref[e]

    def x_copy(b, slot):
        return pltpu.make_async_copy(x_hbm.at[pl.ds((first + b) * rows, rows)], xbuf.at[slot], sem_in.at[slot])

    def o_copy(blk, slot):
        return pltpu.make_async_copy(obuf.at[slot], o_hbm.at[pl.ds(blk * rows, rows), pl.ds(col0, tn)],
                                     sem_out.at[slot])

    @pl.when(nb > 0)
    def _():
        x_copy(0, 0).start()

    def body(b, carry):
        slot = b & 1
        x_copy(b, slot).wait()

        @pl.when(b + 1 < nb)
        def _():
            x_copy(b + 1, 1 - slot).start()

        @pl.when(b >= 2)
        def _():
            o_copy(first + b - 2, slot).wait()

        obuf[slot] = compute(xbuf[slot])
        o_copy(first + b, slot).start()
        return carry

    lax.fori_loop(0, nb, body, 0)

    @pl.when(nb >= 2)
    def _():
        o_copy(first + nb - 2, nb & 1).wait()

    @pl.when(nb >= 1)
    def _():
        o_copy(first + nb - 1, (nb - 1) & 1).wait()

    @pl.when(e == n_experts - 1)
    def _():
        total = o_hbm.shape[0] // rows
        obuf[0] = jnp.zeros(obuf.shape[1:], obuf.dtype)

        def clear(blk, carry):
            cp = o_copy(blk, 0)
            cp.start()
            cp.wait()
            return carry

        lax.fori_loop(nu_ref[0], total, clear, 0)


def _moe_gu_kernel(start_ref, cnt_ref, nu_ref, x_hbm, wg_ref, wl_ref, bg_ref, bl_ref, h_hbm,
                   xbuf, hbuf, wgb_ref, wlb_ref, sem_in, sem_out):
    e = pl.program_id(0)
    j = pl.program_id(1)
    tn = hbuf.shape[2]

    @pl.when(cnt_ref[e] > 0)
    def _():
        wgb_ref[...] = wg_ref[0].astype(BF16)
        wlb_ref[...] = wl_ref[0].astype(BF16)

    def compute(x):
        glu = jnp.minimum(_dot(x, wgb_ref[...]) + bg_ref[0], SWIGLU_LIMIT)
        lin = jnp.clip(_dot(x, wlb_ref[...]) + bl_ref[0], -SWIGLU_LIMIT, SWIGLU_LIMIT)
        return (glu * _sigmoid(SWIGLU_ALPHA * glu) * (lin + 1.0)).astype(BF16)

    _expert_rows(e, pl.num_programs(0), start_ref, cnt_ref, nu_ref, x_hbm, h_hbm, pl.multiple_of(j * tn, tn),
                 xbuf, hbuf, sem_in, sem_out, compute)


def moe_gate_up(x_sorted, eb_start, eb_cnt, n_used, w_gu, b_gu3, *, tn):
    r, d = x_sorted.shape
    ne, _, two_ff = w_gu.shape
    d_ff = two_ff // 2
    nj = d_ff // tn
    gs = pltpu.PrefetchScalarGridSpec(
        num_scalar_prefetch=3,
        grid=(ne, nj),
        in_specs=[pl.BlockSpec(memory_space=pl.ANY),
                  pl.BlockSpec((1, d, tn), lambda e, j, st, cnt, nu: (e, 0, j)),
                  pl.BlockSpec((1, d, tn), lambda e, j, st, cnt, nu: (e, 0, nj + j)),
                  pl.BlockSpec((1, 1, tn), lambda e, j, st, cnt, nu: (e, 0, j)),
                  pl.BlockSpec((1, 1, tn), lambda e, j, st, cnt, nu: (e, 0, nj + j))],
        out_specs=pl.BlockSpec(memory_space=pl.ANY),
        scratch_shapes=[pltpu.VMEM((2, MOE_ROWS, d), BF16), pltpu.VMEM((2, MOE_ROWS, tn), BF16),
                        pltpu.VMEM((d, tn), BF16), pltpu.VMEM((d, tn), BF16),
                        pltpu.SemaphoreType.DMA((2,)), pltpu.SemaphoreType.DMA((2,))])
    return pl.pallas_call(
        _moe_gu_kernel, grid_spec=gs,
        out_shape=jax.ShapeDtypeStruct((r, d_ff), BF16),
        compiler_params=_cparams(2, 40),
        name="moe_gate_up",
    )(eb_start, eb_cnt, n_used, x_sorted, w_gu, w_gu, b_gu3, b_gu3)


def _moe_down_kernel(start_ref, cnt_ref, nu_ref, h_hbm, w_ref, b_ref, o_hbm, hbuf, obuf, wb_ref, sem_in, sem_out):
    e = pl.program_id(0)
    j = pl.program_id(1)
    tn = obuf.shape[2]

    @pl.when(cnt_ref[e] > 0)
    def _():
        wb_ref[...] = w_ref[0].astype(BF16)

    def compute(h):
        return _dot(h, wb_ref[...]) + b_ref[0]

    _expert_rows(e, pl.num_programs(0), start_ref, cnt_ref, nu_ref, h_hbm, o_hbm, pl.multiple_of(j * tn, tn),
                 hbuf, obuf, sem_in, sem_out, compute)


def moe_down(h_sorted, eb_start, eb_cnt, n_used, w_down, b_down3, *, tn):
    r, d_ff = h_sorted.shape
    ne, _, d = w_down.shape
    nj = d // tn
    gs = pltpu.PrefetchScalarGridSpec(
        num_scalar_prefetch=3,
        grid=(ne, nj),
        in_specs=[pl.BlockSpec(memory_space=pl.ANY),
                  pl.BlockSpec((1, d_ff, tn), lambda e, j, st, cnt, nu: (e, 0, j)),
                  pl.BlockSpec((1, 1, tn), lambda e, j, st, cnt, nu: (e, 0, j))],
        out_specs=pl.BlockSpec(memory_space=pl.ANY),
        scratch_shapes=[pltpu.VMEM((2, MOE_ROWS, d_ff), BF16), pltpu.VMEM((2, MOE_ROWS, tn), F32),
                        pltpu.VMEM((d_ff, tn), BF16),
                        pltpu.SemaphoreType.DMA((2,)), pltpu.SemaphoreType.DMA((2,))])
    return pl.pallas_call(
        _moe_down_kernel, grid_spec=gs,
        out_shape=jax.ShapeDtypeStruct((r, d), F32),
        compiler_params=_cparams(2, 40),
        name="moe_down",
    )(eb_start, eb_cnt, n_used, h_sorted, w_down, b_down3)


def _route(logits):
    n = logits.shape[0]
    a = n * TOP_K
    top_val, top_idx = lax.top_k(logits, TOP_K)
    gate = jax.nn.softmax(top_val, axis=-1)
    flat_e = top_idx.reshape(a).astype(jnp.int32)
    experts = jnp.arange(N_EXPERTS, dtype=jnp.int32)
    order = jnp.argsort(flat_e).astype(jnp.int32)
    counts = jnp.sum((flat_e[:, None] == experts[None, :]).astype(jnp.int32), axis=0)
    starts = jnp.cumsum(counts) - counts
    padded = -(-counts // MOE_ROWS) * MOE_ROWS
    pad_ends = jnp.cumsum(padded)
    pad_starts = pad_ends - padded
    sorted_e = flat_e[order]
    dest = pad_starts[sorted_e] + jnp.arange(a, dtype=jnp.int32) - starts[sorted_e]
    pos = dest[jnp.argsort(order)].reshape(n, TOP_K)
    nb = -(-(a + N_EXPERTS * (MOE_ROWS - 1)) // MOE_ROWS)
    rows = jnp.arange(nb * MOE_ROWS, dtype=jnp.int32)
    row_e = jnp.minimum(jnp.sum((pad_ends[None, :] <= rows[:, None]).astype(jnp.int32), axis=1), N_EXPERTS - 1)
    off = rows - pad_starts[row_e]
    src = jnp.clip(starts[row_e] + off, 0, a - 1)
    row_tok = jnp.where(off < counts[row_e], order[src] // TOP_K, 0).astype(jnp.int32)
    eb_start = (pad_starts // MOE_ROWS).astype(jnp.int32)
    eb_cnt = (padded // MOE_ROWS).astype(jnp.int32)
    n_used = (pad_ends[-1] // MOE_ROWS).astype(jnp.int32).reshape(1)
    return row_tok, gate, eb_start, eb_cnt, n_used, pos


def _forward(x_prompt, x_sample, cache_k, cache_v, cache_logf, state_gdn, state_conv, page_table,
             c_prompt, c_sample, w_ada, b_ada, rms_mix, rms_ffn, w_in, b_forget, q_norm, k_norm,
             conv_w, a_log, dt_bias, gdn_norm, w_out, w_router, b_router, w_gu, b_gu, w_down, b_down):
    depth = w_ada.shape[0]
    assert depth == 1, "single-layer trunk"
    l = 0
    bp, s, d = x_prompt.shape
    bd, t, _ = x_sample.shape
    n_pool = cache_k.shape[1]
    tm_p = 256

    n_c = bp + bd
    c_all = jnp.concatenate([c_prompt, c_sample, jnp.zeros((-n_c % SUBLANES, d), F32)], axis=0)
    mod = ada_mod(c_all, w_ada[l], b_ada[l])
    modp = mod[:bp].reshape(bp, 1, N_ADA, d)
    mods = mod[bp:n_c].reshape(bd, 1, N_ADA, d)

    w = w_in[l]
    o_ff, o_g, o_ab = 3 * D_HEADS, 3 * D_HEADS + N_HEADS, 7 * D_HEADS + N_HEADS
    w_small = jnp.concatenate([w[:, o_ff:o_g], w[:, o_ab:o_ab + 2 * N_HEADS]], axis=1)
    w_small = jnp.pad(w_small, ((0, 0), (0, LANES - 3 * N_HEADS)))
    w_fox = jnp.concatenate([w[:, :o_ff], w_small], axis=1).astype(BF16)
    w_g = w[:, o_g:o_ab].astype(BF16)
    qn_t = jnp.tile(q_norm[l], N_HEADS).reshape(1, D_HEADS)
    kn_t = jnp.tile(k_norm[l], N_HEADS).reshape(1, D_HEADS)
    pvec = jnp.zeros((SUBLANES, LANES), F32)
    pvec = pvec.at[0, 0:N_HEADS].set(b_forget[l]).at[0, N_HEADS:2 * N_HEADS].set(dt_bias[l])
    pvec = pvec.at[1, N_HEADS:2 * N_HEADS].set(a_log[l])
    rw1 = rms_mix[l].reshape(1, d)
    rw2 = rms_ffn[l].reshape(1, d)
    w_top = w_out[l][:D_HEADS].astype(BF16)
    w_bot = w_out[l][D_HEADS:].astype(BF16)
    w_router_p = jnp.pad(w_router[l], ((0, 0), (0, LANES - N_EXPERTS)))
    b_router_p = jnp.pad(b_router[l], (0, LANES - N_EXPERTS)).reshape(1, LANES)
    nw = gdn_norm[l].reshape(1, HEAD_DIM)

    def mixer_inputs(x3, m4, conv_in, group, tm):
        sh1, sc1 = m4[:, :, 0], m4[:, :, 1]
        pf = proj_fox(x3, sc1, sh1, rw1, w_fox, qn_t, kn_t, pvec, group=group, tm=tm)
        pg = proj_gdn(x3, sc1, sh1, rw1, w_g, conv_w[l], conv_in, group=group, tm=tm)
        return pf, pg

    pf, pg = mixer_inputs(x_prompt, modp, jnp.zeros((bp, SUBLANES, 3 * D_HEADS), F32), 1, tm_p)
    qb_p, k_p, v_p, kb_p, vb_p, small_p, cum_p = pf
    gq_p, gk_p, gv_p, gz_p, cst_p = pg
    f_t = cum_p[..., :N_HEADS].transpose(0, 2, 1).reshape(bp * N_HEADS, s)
    o_fox_p = fox_prompt_attention(qb_p, kb_p, vb_p, f_t.reshape(bp * N_HEADS, s, 1),
                                   f_t.reshape(bp * N_HEADS, 1, s), tq=512)
    o_gdn_p, s_p = gdn_chunks(gq_p, gk_p, gv_p, gz_p, small_p, cum_p,
                              jnp.zeros((bp, N_HEADS, HEAD_DIM, HEAD_DIM), F32), nw)
    x1_p, h2_p, lg_p = out_proj(x_prompt, o_fox_p, o_gdn_p, modp[:, :, 2], modp[:, :, 4], modp[:, :, 3],
                                rw2, w_top, w_bot, w_router_p, b_router_p, group=1, tm=tm_p)

    conv_in = jnp.pad(state_conv[l], ((0, 0), (SUBLANES - (CONV_W - 1), 0), (0, 0)))
    sf, sg = mixer_inputs(x_sample, mods, conv_in, bd, t)
    qb_s, k_s, v_s, kb_s, vb_s, small_s, cum_s = sf
    gq_s, gk_s, gv_s, gz_s, cst_s = sg
    hq = N_HEADS * t
    cs = cum_s[..., :N_HEADS]
    csn = jnp.pad(cs.reshape(bd, 1, hq), ((0, 0), (0, 0), (0, LANES - hq)))
    csq = cs.transpose(0, 2, 1).reshape(bd, hq, 1)
    q_rows = qb_s.reshape(bd, t, N_HEADS, HEAD_DIM).transpose(0, 2, 1, 3).reshape(bd, hq, HEAD_DIM)
    new_rows = lambda a: jnp.pad(a.reshape(bd, hq, HEAD_DIM), ((0, 0), (0, LANES - hq), (0, 0)))
    lf_rows = cache_logf[l].reshape(n_pool, 1, PAGE_SIZE * N_HEADS)
    o_rows = fox_sample_attention(q_rows, cache_k, cache_v, lf_rows, page_table, new_rows(kb_s), new_rows(vb_s),
                                  csn, csq, pp=8, t=t)
    o_fox_s = o_rows.reshape(bd, N_HEADS, t, HEAD_DIM).transpose(0, 2, 1, 3).reshape(bd, t, D_HEADS)
    padc = lambda a: jnp.pad(a, ((0, 0), (0, GDN_CHUNK - t), (0, 0)))
    cum_s_pad = jnp.pad(cum_s, ((0, 0), (0, GDN_CHUNK - t), (0, 0)), mode="edge")
    o_gdn_s, s_s = gdn_chunks(padc(gq_s), padc(gk_s), padc(gv_s), padc(gz_s), padc(small_s), cum_s_pad,
                              state_gdn[l].astype(F32), nw)
    x1_s, h2_s, lg_s = out_proj(x_sample, o_fox_s, o_gdn_s[:, :t], mods[:, :, 2], mods[:, :, 4], mods[:, :, 3],
                                rw2, w_top, w_bot, w_router_p, b_router_p, group=bd, tm=t)

    n_p, n_s = bp * s, bd * t
    h2_all = jnp.concatenate([h2_p.reshape(n_p, d), h2_s.reshape(n_s, d)], axis=0)
    logits = jnp.concatenate([lg_p.reshape(n_p, LANES), lg_s.reshape(n_s, LANES)], axis=0)[:, :N_EXPERTS]
    row_tok, gate, eb_start, eb_cnt, n_used, pos = _route(logits)
    x_sorted = moe_dispatch(h2_all, row_tok, n_used)
    hid = moe_gate_up(x_sorted, eb_start, eb_cnt, n_used, w_gu[l], b_gu[l].reshape(N_EXPERTS, 1, -1), tn=512)
    outg = moe_down(hid, eb_start, eb_cnt, n_used, w_down[l], b_down[l].reshape(N_EXPERTS, 1, -1), tn=1024)
    moe = sum(gate[:, k:k + 1] * outg[pos[:, k]] for k in range(TOP_K))
    y_p = x1_p + modp[:, :, 5] * moe[:n_p].reshape(bp, s, d)
    y_s = x1_s + mods[:, :, 5] * moe[n_p:].reshape(bd, t, d)

    hd = lambda a, b_, t_: a.reshape(1, b_, t_, N_HEADS, HEAD_DIM)
    return (y_p, y_s,
            hd(k_p, bp, s), hd(v_p, bp, s), small_p[..., :N_HEADS][None],
            s_p[None], cst_p[:, SUBLANES - (CONV_W - 1):][None],
            hd(k_s, bd, t), hd(v_s, bd, t), small_s[..., :N_HEADS][None],
            s_s[None], cst_s[:, SUBLANES - (CONV_W - 1):][None])


def kernel(x_prompt, x_sample, cache_k, cache_v, cache_logf, state_gdn, state_conv, page_table, c_prompt, c_sample, w_ada, b_ada, rms_mix, rms_ffn, w_in, b_forget, q_norm, k_norm, conv_w, a_log, dt_bias, gdn_norm, w_out, w_router, b_router, w_gu, b_gu, w_down, b_down):
    return _forward(x_prompt, x_sample, cache_k, cache_v, cache_logf, state_gdn, state_conv, page_table,
                    c_prompt, c_sample, w_ada, b_ada, rms_mix, rms_ffn, w_in, b_forget, q_norm, k_norm,
                    conv_w, a_log, dt_bias, gdn_norm, w_out, w_router, b_router, w_gu, b_gu, w_down, b_down)
```

```python
import functools

import jax
import jax.numpy as jnp
from jax import lax
from jax.experimental import pallas as pl
from jax.experimental.pallas import tpu as pltpu

F32 = jnp.float32
BF16 = jnp.bfloat16

D_MODEL = 2048
HEAD_DIM = 128
N_HEADS = 8
D_HEADS = N_HEADS * HEAD_DIM
CONV_W = 4
PAGE_SIZE = 128
N_EXPERTS = 32
TOP_K = 4
N_ADA = 6
RMS_EPS = 1e-6
L2_EPS = 1e-6
SWIGLU_ALPHA = 1.702
SWIGLU_LIMIT = 7.0

LANES = 128
SUBLANES = 8
GDN_CHUNK = 128
MOE_ROWS = 256
NEG_BIG = -1e30

_ARB = "arbitrary"


def _cparams(n_axes, vmem_mb):
    return pltpu.CompilerParams(dimension_semantics=(_ARB,) * n_axes,
                                vmem_limit_bytes=vmem_mb * 1024 * 1024)


def _dot(a, b):
    return jnp.dot(a, b, preferred_element_type=F32)


def _dot_nt(a, b):
    return lax.dot_general(a, b, (((1,), (1,)), ((), ())), preferred_element_type=F32)


def _dot_tn(a, b):
    return lax.dot_general(a, b, (((0,), (0,)), ((), ())), preferred_element_type=F32)


def _bmm(a, b):
    return lax.dot_general(a, b, (((2,), (1,)), ((0,), (0,))), preferred_element_type=F32)


def _bmm_nt(a, b):
    return lax.dot_general(a, b, (((2,), (2,)), ((0,), (0,))), preferred_element_type=F32)


def _softplus(z):
    return jnp.maximum(z, 0.0) + jnp.log(1.0 + jnp.exp(-jnp.abs(z)))


def _sigmoid(z):
    return 1.0 / (1.0 + jnp.exp(-z))


def _ada_kernel(c_ref, w_ref, b_ref, o_ref):
    c = c_ref[...]
    a = (c * _sigmoid(c)).astype(BF16)
    o_ref[...] = _dot(a, w_ref[...].astype(BF16)) + b_ref[...]


def ada_mod(c_all, w_ada, b_ada):
    m, d = c_all.shape
    n = w_ada.shape[1]
    tn = min(1024, n)
    return pl.pallas_call(
        _ada_kernel,
        grid=(n // tn,),
        in_specs=[pl.BlockSpec((m, d), lambda j: (0, 0)),
                  pl.BlockSpec((d, tn), lambda j: (0, j)),
                  pl.BlockSpec((1, tn), lambda j: (0, j))],
        out_specs=pl.BlockSpec((m, tn), lambda j: (0, j)),
        out_shape=jax.ShapeDtypeStruct((m, n), F32),
        compiler_params=_cparams(1, 40),
        name="ada_mod",
    )(c_all, w_ada, b_ada.reshape(1, n))


def _mod_norm_bf16(x_ref, sc_ref, sh_ref, rw_ref):
    x = x_ref[...]
    g, tm, d = x.shape
    ms = jnp.mean(x * x, axis=-1, keepdims=True)
    h = x * lax.rsqrt(ms + RMS_EPS) * rw_ref[...]
    h = h * (1.0 + sc_ref[...]) + sh_ref[...]
    return h.reshape(g * tm, d).astype(BF16)


def _per_head(p, fn):
    return jnp.concatenate([fn(p[:, h * HEAD_DIM:(h + 1) * HEAD_DIM]) for h in range(N_HEADS)], axis=-1)


def _rms_head(s):
    return s * lax.rsqrt(jnp.mean(s * s, axis=-1, keepdims=True) + RMS_EPS)


def _l2_head(s):
    return s * lax.rsqrt(jnp.sum(s * s, axis=-1, keepdims=True) + L2_EPS)


def _proj_fox_kernel(x_ref, sc_ref, sh_ref, rw_ref, w_ref, qn_ref, kn_ref, pv_ref,
                     q_ref, k_ref, v_ref, kb_ref, vb_ref, small_ref, cum_ref, carry_ref, *, seg):
    i = pl.program_id(1)
    g, tm, _ = x_ref.shape
    m = g * tm
    hb = _mod_norm_bf16(x_ref, sc_ref, sh_ref, rw_ref)

    qn = _per_head(_dot(hb, w_ref[:, 0:D_HEADS]), _rms_head) * qn_ref[...]
    q_ref[...] = (qn * HEAD_DIM ** -0.5).astype(BF16).reshape(g, tm, D_HEADS)
    kn = _per_head(_dot(hb, w_ref[:, D_HEADS:2 * D_HEADS]), _rms_head) * kn_ref[...]
    k_ref[...] = kn.reshape(g, tm, D_HEADS)
    kb_ref[...] = kn.astype(BF16).reshape(g, tm, D_HEADS)
    vv = _dot(hb, w_ref[:, 2 * D_HEADS:3 * D_HEADS])
    v_ref[...] = vv.reshape(g, tm, D_HEADS)
    vb_ref[...] = vv.astype(BF16).reshape(g, tm, D_HEADS)

    z = _dot(hb, w_ref[:, 3 * D_HEADS:3 * D_HEADS + LANES]) + pv_ref[0:1, :]
    lane = lax.broadcasted_iota(jnp.int32, (m, LANES), 1)
    sp = _softplus(z)
    logsig = z - sp
    gval = -jnp.exp(pv_ref[1:2, :]) * sp
    is_f = lane < N_HEADS
    is_g = (lane >= N_HEADS) & (lane < 2 * N_HEADS)
    is_b = (lane >= 2 * N_HEADS) & (lane < 3 * N_HEADS)
    small = jnp.where(is_f, logsig, jnp.where(is_g, gval, jnp.where(is_b, _sigmoid(z), 0.0)))
    small_ref[...] = small.reshape(g, tm, LANES)

    row = lax.broadcasted_iota(jnp.int32, (m, LANES), 0)
    pos = row & (tm - 1)
    rowmod = jnp.where(is_g, pos & (seg - 1), pos)
    c = small
    s = 1
    while s < tm:
        c = c + jnp.where(rowmod >= s, pltpu.roll(c, s, axis=0), 0.0)
        s *= 2
    lane3 = lax.broadcasted_iota(jnp.int32, (g, tm, LANES), 2)
    carry = jnp.where(i == 0, 0.0, carry_ref[...])
    c3 = c.reshape(g, tm, LANES) + jnp.where(lane3 < N_HEADS, carry, 0.0)
    cum_ref[...] = c3
    carry_ref[...] = c3[:, tm - 1:tm, :]


def proj_fox(x3, sc, sh, rms_w, w_fox, qn_t, kn_t, pvec, *, group, tm):
    nseq, t, d = x3.shape
    g = group
    seg = min(GDN_CHUNK, tm)
    grid = (nseq // g, t // tm)
    blk = lambda w: pl.BlockSpec((g, tm, w), lambda s, i: (s, i, 0))
    const2 = lambda a: pl.BlockSpec(a.shape, lambda s, i: (0, 0))
    mod = pl.BlockSpec((g, 1, d), lambda s, i: (s, 0, 0))
    outs = [jax.ShapeDtypeStruct((nseq, t, D_HEADS), BF16),
            jax.ShapeDtypeStruct((nseq, t, D_HEADS), F32),
            jax.ShapeDtypeStruct((nseq, t, D_HEADS), F32),
            jax.ShapeDtypeStruct((nseq, t, D_HEADS), BF16),
            jax.ShapeDtypeStruct((nseq, t, D_HEADS), BF16),
            jax.ShapeDtypeStruct((nseq, t, LANES), F32),
            jax.ShapeDtypeStruct((nseq, t, LANES), F32)]
    return pl.pallas_call(
        functools.partial(_proj_fox_kernel, seg=seg),
        grid=grid,
        in_specs=[blk(d), mod, mod, const2(rms_w),
                  pl.BlockSpec(w_fox.shape, lambda s, i: (0, 0), pipeline_mode=pl.Buffered(1)),
                  const2(qn_t), const2(kn_t), const2(pvec)],
        out_specs=[blk(D_HEADS)] * 5 + [blk(LANES)] * 2,
        out_shape=outs,
        scratch_shapes=[pltpu.VMEM((g, 1, LANES), F32)],
        compiler_params=_cparams(2, 48),
        name="proj_fox",
    )(x3, sc, sh, rms_w, w_fox, qn_t, kn_t, pvec)


def _proj_gdn_kernel(x_ref, sc_ref, sh_ref, rw_ref, w_ref, cw_ref, cin_ref,
                     q_ref, k_ref, v_ref, z_ref, cst_ref, cbuf_ref):
    i = pl.program_id(1)
    g, tm, _ = x_ref.shape
    hb = _mod_norm_bf16(x_ref, sc_ref, sh_ref, rw_ref)
    outs = (q_ref, k_ref, v_ref)
    for c in range(3):
        cols = slice(c * D_HEADS, (c + 1) * D_HEADS)
        pc = _dot(hb, w_ref[:, cols]).reshape(g, tm, D_HEADS)

        @pl.when(i == 0)
        def _():
            cbuf_ref[c, :, 0:SUBLANES, :] = cin_ref[:, :, cols]

        @pl.when(i > 0)
        def _():
            cbuf_ref[c, :, 0:SUBLANES, :] = cbuf_ref[c, :, tm:tm + SUBLANES, :]

        cbuf_ref[c, :, SUBLANES:tm + SUBLANES, :] = pc
        cst_ref[:, :, cols] = pc[:, tm - SUBLANES:tm, :]
        base = SUBLANES - (CONV_W - 1)
        y = None
        for j in range(CONV_W):
            term = cbuf_ref[c, :, base + j:base + j + tm, :] * cw_ref[j:j + 1, cols]
            y = term if y is None else y + term
        y = (y * _sigmoid(y)).reshape(g * tm, D_HEADS)
        if c == 0:
            y = _per_head(y, _l2_head) * HEAD_DIM ** -0.5
        elif c == 1:
            y = _per_head(y, _l2_head)
        outs[c][...] = y.reshape(g, tm, D_HEADS)
    z_ref[...] = _dot(hb, w_ref[:, 3 * D_HEADS:4 * D_HEADS]).reshape(g, tm, D_HEADS)


def proj_gdn(x3, sc, sh, rms_w, w_g, conv_w, conv_in, *, group, tm):
    nseq, t, d = x3.shape
    g = group
    grid = (nseq // g, t // tm)
    blk = lambda w: pl.BlockSpec((g, tm, w), lambda s, i: (s, i, 0))
    const2 = lambda a: pl.BlockSpec(a.shape, lambda s, i: (0, 0))
    mod = pl.BlockSpec((g, 1, d), lambda s, i: (s, 0, 0))
    st = pl.BlockSpec((g, SUBLANES, 3 * D_HEADS), lambda s, i: (s, 0, 0))
    outs = [jax.ShapeDtypeStruct((nseq, t, D_HEADS), F32)] * 4 + \
           [jax.ShapeDtypeStruct((nseq, SUBLANES, 3 * D_HEADS), F32)]
    return pl.pallas_call(
        _proj_gdn_kernel,
        grid=grid,
        in_specs=[blk(d), mod, mod, const2(rms_w),
                  pl.BlockSpec(w_g.shape, lambda s, i: (0, 0), pipeline_mode=pl.Buffered(1)),
                  const2(conv_w), st],
        out_specs=[blk(D_HEADS)] * 4 + [st],
        out_shape=outs,
        scratch_shapes=[pltpu.VMEM((3, g, tm + SUBLANES, D_HEADS), F32)],
        compiler_params=_cparams(2, 52),
        name="proj_gdn",
    )(x3, sc, sh, rms_w, w_g, conv_w, conv_in)


def _fox_prompt_kernel(q_ref, k_ref, v_ref, fc_ref, fr_ref, o_ref, *, tq):
    qi = pl.program_id(2)
    q = q_ref[0]
    fq = fc_ref[0]

    def step(j, carry, masked):
        m_i, l_i, acc = carry
        off = pl.multiple_of(j * tq, tq)
        ks = k_ref[0, pl.ds(off, tq), :]
        vs = v_ref[0, pl.ds(off, tq), :]
        s = _dot_nt(q, ks) + (fq - fr_ref[0, :, pl.ds(off, tq)])
        if masked:
            r = lax.broadcasted_iota(jnp.int32, (tq, tq), 0)
            c = lax.broadcasted_iota(jnp.int32, (tq, tq), 1)
            s = jnp.where(c <= r, s, NEG_BIG)
        m_new = jnp.maximum(m_i, jnp.max(s, axis=-1, keepdims=True))
        alpha = jnp.exp(m_i - m_new)
        p = jnp.exp(s - m_new)
        l_new = alpha * l_i + jnp.sum(p, axis=-1, keepdims=True)
        acc = alpha * acc + _dot(p.astype(BF16), vs)
        return m_new, l_new, acc

    init = (jnp.full((tq, 1), NEG_BIG, F32), jnp.zeros((tq, 1), F32), jnp.zeros((tq, HEAD_DIM), F32))
    carry = lax.fori_loop(0, qi, functools.partial(step, masked=False), init)
    _, l_i, acc = step(qi, carry, True)
    o_ref[0] = (acc / l_i).astype(BF16)


def fox_prompt_attention(qb, kb, vb, f_col, f_row, *, tq):
    b, s, _ = qb.shape
    grid = (b, N_HEADS, s // tq)
    return pl.pallas_call(
        functools.partial(_fox_prompt_kernel, tq=tq),
        grid=grid,
        in_specs=[pl.BlockSpec((1, tq, HEAD_DIM), lambda bi, h, qi: (bi, qi, h)),
                  pl.BlockSpec((1, s, HEAD_DIM), lambda bi, h, qi: (bi, 0, h)),
                  pl.BlockSpec((1, s, HEAD_DIM), lambda bi, h, qi: (bi, 0, h)),
                  pl.BlockSpec((1, tq, 1), lambda bi, h, qi: (bi * N_HEADS + h, qi, 0)),
                  pl.BlockSpec((1, 1, s), lambda bi, h, qi: (bi * N_HEADS + h, 0, 0))],
        out_specs=pl.BlockSpec((1, tq, HEAD_DIM), lambda bi, h, qi: (bi, qi, h)),
        out_shape=jax.ShapeDtypeStruct((b, s, D_HEADS), BF16),
        compiler_params=_cparams(3, 32),
        name="fox_prompt_attention",
    )(qb, kb, vb, f_col, f_row)


def _fox_sample_kernel(pt_ref, q_ref, *refs, pp, t):
    k_refs = refs[0:pp]
    v_refs = refs[pp:2 * pp]
    lf_refs = refs[2 * pp:3 * pp]
    kn_ref, vn_ref, csn_ref, csq_ref, o_ref, m_ref, l_ref, acc_ref, carry_ref = refs[3 * pp:]
    s = pl.program_id(1)
    ns = pl.num_programs(1)
    hq = N_HEADS * t
    ncol = PAGE_SIZE * N_HEADS

    @pl.when(s == 0)
    def _():
        m_ref[...] = jnp.full(m_ref.shape, NEG_BIG, F32)
        l_ref[...] = jnp.zeros(l_ref.shape, F32)
        acc_ref[...] = jnp.zeros(acc_ref.shape, F32)
        carry_ref[...] = jnp.zeros(carry_ref.shape, F32)

    q = q_ref[0]
    csq = csq_ref[0]

    def update(sc, vals, width):
        m_old = m_ref[...]
        m_new = jnp.maximum(m_old, jnp.max(sc, axis=-1, keepdims=True))
        alpha = jnp.exp(m_old - m_new)
        p = jnp.exp(sc - m_new)
        l_ref[...] = alpha * l_ref[...] + jnp.sum(p, axis=-1, keepdims=True)
        pv = None
        for j, vv in enumerate(vals):
            term = _dot(p[:, j * width:(j + 1) * width].astype(BF16), vv)
            pv = term if pv is None else pv + term
        acc_ref[...] = alpha * acc_ref[...] + pv
        m_ref[...] = m_new

    lane = lax.broadcasted_iota(jnp.int32, (1, ncol), 1)
    carry = carry_ref[...]
    bias = [None] * pp
    for j in reversed(range(pp)):
        lf = lf_refs[j][...]
        c = lf
        step = N_HEADS
        while step < ncol:
            c = c + jnp.where(lane + step < ncol, pltpu.roll(c, ncol - step, axis=1), 0.0)
            step *= 2
        bias[j] = c - lf + carry
        tot = jnp.where(lane < N_HEADS, c, 0.0)
        step = N_HEADS
        while step < ncol:
            tot = tot + pltpu.roll(tot, step, axis=1)
            step *= 2
        carry = carry + tot
    carry_ref[...] = carry

    row = lax.broadcasted_iota(jnp.int32, (hq, ncol), 0)
    col = lax.broadcasted_iota(jnp.int32, (hq, ncol), 1)
    t_shift = t.bit_length() - 1
    h_shift = N_HEADS.bit_length() - 1
    same_head = (col & (N_HEADS - 1)) == (row >> t_shift)
    sc = [jnp.where(same_head, _dot_nt(q, k_refs[j][...].reshape(ncol, HEAD_DIM).astype(BF16)) + (csq + bias[j]),
                    NEG_BIG) for j in range(pp)]
    update(jnp.concatenate(sc, axis=1),
           [v_refs[j][...].reshape(ncol, HEAD_DIM).astype(BF16) for j in range(pp)], ncol)

    @pl.when(s == ns - 1)
    def _():
        rn = lax.broadcasted_iota(jnp.int32, (hq, LANES), 0)
        cn = lax.broadcasted_iota(jnp.int32, (hq, LANES), 1)
        ok = ((cn & (N_HEADS - 1)) == (rn >> t_shift)) & ((cn >> h_shift) <= (rn & (t - 1)))
        sn = jnp.where(ok, _dot_nt(q, kn_ref[0]) + (csq - csn_ref[0]), NEG_BIG)
        update(sn, [vn_ref[0]], LANES)
        o_ref[0] = (acc_ref[...] / l_ref[...]).astype(BF16)


def fox_sample_attention(q_rows, cache_k, cache_v, lf_rows, page_table, kn_pad, vn_pad, csn, csq, *, pp, t):
    bd, hq, _ = q_rows.shape
    n_pages = page_table.shape[1]
    ns = n_pages // pp
    ncol = PAGE_SIZE * N_HEADS
    page_of = lambda b, s, j, pt: pt[b, (ns - 1 - s) * pp + j]
    kv_spec = lambda j: pl.BlockSpec((None, None, PAGE_SIZE, N_HEADS, HEAD_DIM),
                                     lambda b, s, pt: (0, page_of(b, s, j, pt), 0, 0, 0))
    lf_spec = lambda j: pl.BlockSpec((None, 1, ncol), lambda b, s, pt: (page_of(b, s, j, pt), 0, 0))
    per_b = lambda shp: pl.BlockSpec((1,) + shp, lambda b, s, pt: (b, 0, 0))
    gs = pltpu.PrefetchScalarGridSpec(
        num_scalar_prefetch=1,
        grid=(bd, ns),
        in_specs=[per_b((hq, HEAD_DIM))] + [kv_spec(j) for j in range(pp)] * 2 + [lf_spec(j) for j in range(pp)] +
                 [per_b((LANES, HEAD_DIM)), per_b((LANES, HEAD_DIM)), per_b((1, LANES)), per_b((hq, 1))],
        out_specs=per_b((hq, HEAD_DIM)),
        scratch_shapes=[pltpu.VMEM((hq, 1), F32), pltpu.VMEM((hq, 1), F32), pltpu.VMEM((hq, HEAD_DIM), F32),
                        pltpu.VMEM((1, ncol), F32)])
    return pl.pallas_call(
        functools.partial(_fox_sample_kernel, pp=pp, t=t), grid_spec=gs,
        out_shape=jax.ShapeDtypeStruct((bd, hq, HEAD_DIM), BF16),
        compiler_params=_cparams(2, 48),
        name="fox_sample_attention",
    )(page_table, q_rows, *([cache_k] * pp), *([cache_v] * pp), *([lf_rows] * pp), kn_pad, vn_pad, csn, csq)


def _gdn_kernel(q_ref, k_ref, v_ref, z_ref, small_ref, cum_ref, s0_ref, nw_ref, o_ref, so_ref, s_ref):
    ci = pl.program_id(1)
    c = GDN_CHUNK

    @pl.when(ci == 0)
    def _():
        s_ref[...] = s0_ref[0]

    cum = cum_ref[0]
    cum_t = cum.T
    small = small_ref[0]
    r = lax.broadcasted_iota(jnp.int32, (c, c), 0)
    cc = lax.broadcasted_iota(jnp.int32, (c, c), 1)
    tri = cc <= r
    strict = cc < r
    eye = (cc == r).astype(F32)
    pair_masks = []
    shift = 0
    while (1 << shift) < c:
        rb = r >> shift
        pair_masks.append(((rb & 1) == 1) & ((cc >> shift) == rb - 1))
        shift += 1
    heads = range(N_HEADS)
    stack = lambda f: jnp.stack([f(h) for h in heads], axis=0)
    hs = lambda h: slice(h * HEAD_DIM, (h + 1) * HEAD_DIM)
    q = stack(lambda h: q_ref[0, :, hs(h)])
    k = stack(lambda h: k_ref[0, :, hs(h)])
    v = stack(lambda h: v_ref[0, :, hs(h)])
    gc = stack(lambda h: cum[:, N_HEADS + h:N_HEADS + h + 1])
    gr = stack(lambda h: cum_t[N_HEADS + h:N_HEADS + h + 1, :])
    bt = stack(lambda h: small[:, 2 * N_HEADS + h:2 * N_HEADS + h + 1])
    g_last = gc[:, c - 1:c, :]
    decay = jnp.where(tri, jnp.exp(jnp.where(tri, gc - gr, 0.0)), 0.0)
    kb = k * bt
    kk_qk = _bmm_nt(jnp.concatenate([kb, q], axis=1).astype(BF16), k.astype(BF16))
    lmat = jnp.where(strict, kk_qk[:, :c] * decay, 0.0)
    a_in = jnp.where(tri, kk_qk[:, c:] * decay, 0.0)
    tm = jnp.broadcast_to(eye, (N_HEADS, c, c))
    for e_mask in pair_masks:
        tb = tm.astype(BF16)
        te = _bmm(tb, jnp.where(e_mask, lmat, 0.0).astype(BF16))
        tm = tm - _bmm(te.astype(BF16), tb)
    eg = jnp.exp(gc)
    uwb = _bmm(tm.astype(BF16), jnp.concatenate([v * bt, kb * eg], axis=2).astype(BF16)).astype(BF16)
    au = _bmm(a_in.astype(BF16), uwb)
    kt = k * jnp.exp(g_last - gc)
    kt_t = stack(lambda h: kt[h].T)
    ku = _bmm(kt_t.astype(BF16), uwb)
    qt = q * eg - au[:, :, HEAD_DIM:]
    s_all = s_ref[...]
    x = _bmm(jnp.concatenate([-ku[:, :, HEAD_DIM:], qt], axis=1).astype(BF16), s_all.astype(BF16))
    s_new = s_all * jnp.exp(g_last) + x[:, :HEAD_DIM] + ku[:, :, :HEAD_DIM]
    s_ref[...] = s_new
    so_ref[0] = s_new
    o = x[:, HEAD_DIM:] + au[:, :, :HEAD_DIM]
    o = o * lax.rsqrt(jnp.mean(o * o, axis=-1, keepdims=True) + RMS_EPS) * nw_ref[...]
    for h in heads:
        zh = z_ref[0, :, hs(h)]
        o_ref[0, :, hs(h)] = (o[h] * (zh * _sigmoid(zh))).astype(BF16)


def gdn_chunks(q, k, v, z, small, cum, s0, norm_w):
    nseq, tp, _ = q.shape
    c = GDN_CHUNK
    blk = lambda w: pl.BlockSpec((1, c, w), lambda s, i: (s, i, 0))
    st = pl.BlockSpec((1, N_HEADS, HEAD_DIM, HEAD_DIM), lambda s, i: (s, 0, 0, 0))
    return pl.pallas_call(
        _gdn_kernel,
        grid=(nseq, tp // c),
        in_specs=[blk(D_HEADS)] * 4 + [blk(LANES)] * 2 + [st, pl.BlockSpec((1, HEAD_DIM), lambda s, i: (0, 0))],
        out_specs=[blk(D_HEADS), st],
        out_shape=[jax.ShapeDtypeStruct((nseq, tp, D_HEADS), BF16),
                   jax.ShapeDtypeStruct((nseq, N_HEADS, HEAD_DIM, HEAD_DIM), F32)],
        scratch_shapes=[pltpu.VMEM((N_HEADS, HEAD_DIM, HEAD_DIM), F32)],
        compiler_params=_cparams(2, 32),
        name="gdn_chunks",
    )(q, k, v, z, small, cum, s0, norm_w)


def _split3(a):
    hi = a.astype(BF16)
    lo = (a - hi.astype(F32)).astype(BF16)
    return hi, lo


def _out_proj_kernel(x_ref, of_ref, og_ref, g1_ref, sc_ref, sh_ref, rw_ref, wt_ref, wb_ref, wr_ref, br_ref,
                     x1_ref, h2_ref, lg_ref):
    g, tm, d = x_ref.shape
    m = g * tm
    mix = _dot(of_ref[...].reshape(m, D_HEADS), wt_ref[...]) + _dot(og_ref[...].reshape(m, D_HEADS), wb_ref[...])
    x1 = x_ref[...] + g1_ref[...] * mix.reshape(g, tm, d)
    x1_ref[...] = x1
    ms = jnp.mean(x1 * x1, axis=-1, keepdims=True)
    h2 = x1 * lax.rsqrt(ms + RMS_EPS) * rw_ref[...]
    h2 = (h2 * (1.0 + sc_ref[...]) + sh_ref[...]).reshape(m, d)
    h2_ref[...] = h2.reshape(g, tm, d)
    hh, hl = _split3(h2)
    wh, wl = _split3(wr_ref[...])
    lg = _dot(hh, wh) + (_dot(hh, wl) + _dot(hl, wh)) + br_ref[...]
    lg_ref[...] = lg.reshape(g, tm, LANES)


def out_proj(x3, o_fox, o_gdn, g1, sc2, sh2, rms_w, w_top, w_bot, w_router_p, b_router_p, *, group, tm):
    nseq, t, d = x3.shape
    g = group
    blk = lambda w: pl.BlockSpec((g, tm, w), lambda s, i: (s, i, 0))
    const2 = lambda a: pl.BlockSpec(a.shape, lambda s, i: (0, 0))
    res = lambda a: pl.BlockSpec(a.shape, lambda s, i: (0, 0), pipeline_mode=pl.Buffered(1))
    mod = pl.BlockSpec((g, 1, d), lambda s, i: (s, 0, 0))
    return pl.pallas_call(
        _out_proj_kernel,
        grid=(nseq // g, t // tm),
        in_specs=[blk(d), blk(D_HEADS), blk(D_HEADS), mod, mod, mod, const2(rms_w),
                  res(w_top), res(w_bot), const2(w_router_p), const2(b_router_p)],
        out_specs=[blk(d), blk(d), blk(LANES)],
        out_shape=[jax.ShapeDtypeStruct((nseq, t, d), F32), jax.ShapeDtypeStruct((nseq, t, d), F32),
                   jax.ShapeDtypeStruct((nseq, t, LANES), F32)],
        compiler_params=_cparams(2, 40),
        name="out_proj",
    )(x3, o_fox, o_gdn, g1, sc2, sh2, rms_w, w_top, w_bot, w_router_p, b_router_p)


def _row_copy(src_hbm, buf_ref, sem, tok, r):
    return pltpu.make_async_copy(src_hbm.at[pl.ds(tok, 1)], buf_ref.at[pl.ds(r, 1)], sem)


def _moe_dispatch_kernel(tok_ref, nu_ref, src_hbm, o_ref, buf_ref, sem):
    i = pl.program_id(0)
    rows = buf_ref.shape[0]

    @pl.when(i < nu_ref[0])
    def _():
        base = i * rows

        def issue(g, carry):
            for u in range(SUBLANES):
                r = g * SUBLANES + u
                _row_copy(src_hbm, buf_ref, sem, tok_ref[base + r], r).start(priority=u % 2)
            return carry

        lax.fori_loop(0, rows // SUBLANES, issue, 0)
        pltpu.make_async_copy(src_hbm.at[pl.ds(0, rows)], buf_ref, sem).wait()
        o_ref[...] = buf_ref[...].astype(BF16)

    @pl.when(i >= nu_ref[0])
    def _():
        o_ref[...] = jnp.zeros(o_ref.shape, o_ref.dtype)


def moe_dispatch(h2_all, row_tok, n_used):
    n, d = h2_all.shape
    r = row_tok.shape[0]
    gs = pltpu.PrefetchScalarGridSpec(
        num_scalar_prefetch=2,
        grid=(r // MOE_ROWS,),
        in_specs=[pl.BlockSpec(memory_space=pl.ANY)],
        out_specs=pl.BlockSpec((MOE_ROWS, d), lambda i, tok, nu: (i, 0)),
        scratch_shapes=[pltpu.VMEM((MOE_ROWS, d), F32), pltpu.SemaphoreType.DMA(())])
    return pl.pallas_call(
        _moe_dispatch_kernel, grid_spec=gs,
        out_shape=jax.ShapeDtypeStruct((r, d), BF16),
        compiler_params=_cparams(1, 16),
        name="moe_dispatch",
    )(row_tok, n_used, h2_all)


def _expert_rows(e, n_experts, start_ref, cnt_ref, nu_ref, x_hbm, o_hbm, col0, xbuf, obuf, sem_in, sem_out, compute):
    rows = xbuf.shape[1]
    tn = obuf.shape[2]
    nb = cnt_ref[e]
    first = start_ref[e]

    def x_copy(b, slot):
        return pltpu.make_async_copy(x_hbm.at[pl.ds((first + b) * rows, rows)], xbuf.at[slot], sem_in.at[slot])

    def o_copy(blk, slot):
        return pltpu.make_async_copy(obuf.at[slot], o_hbm.at[pl.ds(blk * rows, rows), pl.ds(col0, tn)],
                                     sem_out.at[slot])

    @pl.when(nb > 0)
    def _():
        x_copy(0, 0).start(priority=1)

    def body(b, carry):
        slot = b & 1
        x_copy(b, slot).wait()

        @pl.when(b + 1 < nb)
        def _():
            x_copy(b + 1, 1 - slot).start(priority=1)

        @pl.when(b >= 2)
        def _():
            o_copy(first + b - 2, slot).wait()

        obuf[slot] = compute(xbuf[slot])
        o_copy(first + b, slot).start()
        return carry

    lax.fori_loop(0, nb, body, 0)

    @pl.when(nb >= 2)
    def _():
        o_copy(first + nb - 2, nb & 1).wait()

    @pl.when(nb >= 1)
    def _():
        o_copy(first + nb - 1, (nb - 1) & 1).wait()

    @pl.when(e == n_experts - 1)
    def _():
        total = o_hbm.shape[0] // rows
        obuf[0] = jnp.zeros(obuf.shape[1:], obuf.dtype)

        def clear(blk, carry):
            cp = o_copy(blk, 0)
            cp.start()
            cp.wait()
            return carry

        lax.fori_loop(nu_ref[0], total, clear, 0)


def _moe_gu_kernel(start_ref, cnt_ref, nu_ref, x_hbm, wg_ref, wl_ref, bg_ref, bl_ref, h_hbm,
                   xbuf, hbuf, wgb_ref, wlb_ref, sem_in, sem_out):
    e = pl.program_id(0)
    j = pl.program_id(1)
    tn = hbuf.shape[2]

    @pl.when(cnt_ref[e] > 0)
    def _():
        wgb_ref[...] = wg_ref[0].astype(BF16)
        wlb_ref[...] = wl_ref[0].astype(BF16)

    def compute(x):
        glu = jnp.minimum(_dot(x, wgb_ref[...]) + bg_ref[0], SWIGLU_LIMIT)
        lin = jnp.clip(_dot(x, wlb_ref[...]) + bl_ref[0], -SWIGLU_LIMIT, SWIGLU_LIMIT)
        return (glu * _sigmoid(SWIGLU_ALPHA * glu) * (lin + 1.0)).astype(BF16)

    _expert_rows(e, pl.num_programs(0), start_ref, cnt_ref, nu_ref, x_hbm, h_hbm, pl.multiple_of(j * tn, tn),
                 xbuf, hbuf, sem_in, sem_out, compute)


def moe_gate_up(x_sorted, eb_start, eb_cnt, n_used, w_gu, b_gu3, *, tn):
    r, d = x_sorted.shape
    ne, _, two_ff = w_gu.shape
    d_ff = two_ff // 2
    nj = d_ff // tn
    gs = pltpu.PrefetchScalarGridSpec(
        num_scalar_prefetch=3,
        grid=(ne, nj),
        in_specs=[pl.BlockSpec(memory_space=pl.ANY),
                  pl.BlockSpec((1, d, tn), lambda e, j, st, cnt, nu: (e, 0, j)),
                  pl.BlockSpec((1, d, tn), lambda e, j, st, cnt, nu: (e, 0, nj + j)),
                  pl.BlockSpec((1, 1, tn), lambda e, j, st, cnt, nu: (e, 0, j)),
                  pl.BlockSpec((1, 1, tn), lambda e, j, st, cnt, nu: (e, 0, nj + j))],
        out_specs=pl.BlockSpec(memory_space=pl.ANY),
        scratch_shapes=[pltpu.VMEM((2, MOE_ROWS, d), BF16), pltpu.VMEM((2, MOE_ROWS, tn), BF16),
                        pltpu.VMEM((d, tn), BF16), pltpu.VMEM((d, tn), BF16),
                        pltpu.SemaphoreType.DMA((2,)), pltpu.SemaphoreType.DMA((2,))])
    return pl.pallas_call(
        _moe_gu_kernel, grid_spec=gs,
        out_shape=jax.ShapeDtypeStruct((r, d_ff), BF16),
        compiler_params=_cparams(2, 40),
        name="moe_gate_up",
    )(eb_start, eb_cnt, n_used, x_sorted, w_gu, w_gu, b_gu3, b_gu3)


def _moe_down_kernel(start_ref, cnt_ref, nu_ref, h_hbm, w_ref, b_ref, o_hbm, hbuf, obuf, wb_ref, sem_in, sem_out):
    e = pl.program_id(0)
    j = pl.program_id(1)
    tn = obuf.shape[2]

    @pl.when(cnt_ref[e] > 0)
    def _():
        wb_ref[...] = w_ref[0].astype(BF16)

    def compute(h):
        return _dot(h, wb_ref[...]) + b_ref[0]

    _expert_rows(e, pl.num_programs(0), start_ref, cnt_ref, nu_ref, h_hbm, o_hbm, pl.multiple_of(j * tn, tn),
                 hbuf, obuf, sem_in, sem_out, compute)


def moe_down(h_sorted, eb_start, eb_cnt, n_used, w_down, b_down3, *, tn):
    r, d_ff = h_sorted.shape
    ne, _, d = w_down.shape
    nj = d // tn
    gs = pltpu.PrefetchScalarGridSpec(
        num_scalar_prefetch=3,
        grid=(ne, nj),
        in_specs=[pl.BlockSpec(memory_space=pl.ANY),
                  pl.BlockSpec((1, d_ff, tn), lambda e, j, st, cnt, nu: (e, 0, j)),
                  pl.BlockSpec((1, 1, tn), lambda e, j, st, cnt, nu: (e, 0, j))],
        out_specs=pl.BlockSpec(memory_space=pl.ANY),
        scratch_shapes=[pltpu.VMEM((2, MOE_ROWS, d_ff), BF16), pltpu.VMEM((2, MOE_ROWS, tn), F32),
                        pltpu.VMEM((d_ff, tn), BF16),
                        pltpu.SemaphoreType.DMA((2,)), pltpu.SemaphoreType.DMA((2,))])
    return pl.pallas_call(
        _moe_down_kernel, grid_spec=gs,
        out_shape=jax.ShapeDtypeStruct((r, d), F32),
        compiler_params=_cparams(2, 40),
        name="moe_down",
    )(eb_start, eb_cnt, n_used, h_sorted, w_down, b_down3)


def _route(logits):
    n = logits.shape[0]
    a = n * TOP_K
    top_val, top_idx = lax.top_k(logits, TOP_K)
    gate = jax.nn.softmax(top_val, axis=-1)
    flat_e = top_idx.reshape(a).astype(jnp.int32)
    experts = jnp.arange(N_EXPERTS, dtype=jnp.int32)
    order = jnp.argsort(flat_e).astype(jnp.int32)
    counts = jnp.sum((flat_e[:, None] == experts[None, :]).astype(jnp.int32), axis=0)
    starts = jnp.cumsum(counts) - counts
    padded = -(-counts // MOE_ROWS) * MOE_ROWS
    pad_ends = jnp.cumsum(padded)
    pad_starts = pad_ends - padded
    sorted_e = flat_e[order]
    dest = pad_starts[sorted_e] + jnp.arange(a, dtype=jnp.int32) - starts[sorted_e]
    pos = dest[jnp.argsort(order)].reshape(n, TOP_K)
    nb = -(-(a + N_EXPERTS * (MOE_ROWS - 1)) // MOE_ROWS)
    rows = jnp.arange(nb * MOE_ROWS, dtype=jnp.int32)
    row_e = jnp.minimum(jnp.sum((pad_ends[None, :] <= rows[:, None]).astype(jnp.int32), axis=1), N_EXPERTS - 1)
    off = rows - pad_starts[row_e]
    src = jnp.clip(starts[row_e] + off, 0, a - 1)
    row_tok = jnp.where(off < counts[row_e], order[src] // TOP_K, 0).astype(jnp.int32)
    eb_start = (pad_starts // MOE_ROWS).astype(jnp.int32)
    eb_cnt = (padded // MOE_ROWS).astype(jnp.int32)
    n_used = (pad_ends[-1] // MOE_ROWS).astype(jnp.int32).reshape(1)
    return row_tok, gate, eb_start, eb_cnt, n_used, pos


def _forward(x_prompt, x_sample, cache_k, cache_v, cache_logf, state_gdn, state_conv, page_table,
             c_prompt, c_sample, w_ada, b_ada, rms_mix, rms_ffn, w_in, b_forget, q_norm, k_norm,
             conv_w, a_log, dt_bias, gdn_norm, w_out, w_router, b_router, w_gu, b_gu, w_down, b_down):
    depth = w_ada.shape[0]
    assert depth == 1, "single-layer trunk"
    l = 0
    bp, s, d = x_prompt.shape
    bd, t, _ = x_sample.shape
    n_pool = cache_k.shape[1]
    tm_p = 256

    n_c = bp + bd
    c_all = jnp.concatenate([c_prompt, c_sample, jnp.zeros((-n_c % SUBLANES, d), F32)], axis=0)
    mod = ada_mod(c_all, w_ada[l], b_ada[l])
    modp = mod[:bp].reshape(bp, 1, N_ADA, d)
    mods = mod[bp:n_c].reshape(bd, 1, N_ADA, d)

    w = w_in[l]
    o_ff, o_g, o_ab = 3 * D_HEADS, 3 * D_HEADS + N_HEADS, 7 * D_HEADS + N_HEADS
    w_small = jnp.concatenate([w[:, o_ff:o_g], w[:, o_ab:o_ab + 2 * N_HEADS]], axis=1)
    w_small = jnp.pad(w_small, ((0, 0), (0, LANES - 3 * N_HEADS)))
    w_fox = jnp.concatenate([w[:, :o_ff], w_small], axis=1).astype(BF16)
    w_g = w[:, o_g:o_ab].astype(BF16)
    qn_t = jnp.tile(q_norm[l], N_HEADS).reshape(1, D_HEADS)
    kn_t = jnp.tile(k_norm[l], N_HEADS).reshape(1, D_HEADS)
    pvec = jnp.zeros((SUBLANES, LANES), F32)
    pvec = pvec.at[0, 0:N_HEADS].set(b_forget[l]).at[0, N_HEADS:2 * N_HEADS].set(dt_bias[l])
    pvec = pvec.at[1, N_HEADS:2 * N_HEADS].set(a_log[l])
    rw1 = rms_mix[l].reshape(1, d)
    rw2 = rms_ffn[l].reshape(1, d)
    w_top = w_out[l][:D_HEADS].astype(BF16)
    w_bot = w_out[l][D_HEADS:].astype(BF16)
    w_router_p = jnp.pad(w_router[l], ((0, 0), (0, LANES - N_EXPERTS)))
    b_router_p = jnp.pad(b_router[l], (0, LANES - N_EXPERTS)).reshape(1, LANES)
    nw = gdn_norm[l].reshape(1, HEAD_DIM)

    def mixer_inputs(x3, m4, conv_in, group, tm):
        sh1, sc1 = m4[:, :, 0], m4[:, :, 1]
        pf = proj_fox(x3, sc1, sh1, rw1, w_fox, qn_t, kn_t, pvec, group=group, tm=tm)
        pg = proj_gdn(x3, sc1, sh1, rw1, w_g, conv_w[l], conv_in, group=group, tm=tm)
        return pf, pg

    pf, pg = mixer_inputs(x_prompt, modp, jnp.zeros((bp, SUBLANES, 3 * D_HEADS), F32), 1, tm_p)
    qb_p, k_p, v_p, kb_p, vb_p, small_p, cum_p = pf
    gq_p, gk_p, gv_p, gz_p, cst_p = pg
    f_t = cum_p[..., :N_HEADS].transpose(0, 2, 1).reshape(bp * N_HEADS, s)
    o_fox_p = fox_prompt_attention(qb_p, kb_p, vb_p, f_t.reshape(bp * N_HEADS, s, 1),
                                   f_t.reshape(bp * N_HEADS, 1, s), tq=512)
    o_gdn_p, s_p = gdn_chunks(gq_p, gk_p, gv_p, gz_p, small_p, cum_p,
                              jnp.zeros((bp, N_HEADS, HEAD_DIM, HEAD_DIM), F32), nw)
    x1_p, h2_p, lg_p = out_proj(x_prompt, o_fox_p, o_gdn_p, modp[:, :, 2], modp[:, :, 4], modp[:, :, 3],
                                rw2, w_top, w_bot, w_router_p, b_router_p, group=1, tm=tm_p)

    conv_in = jnp.pad(state_conv[l], ((0, 0), (SUBLANES - (CONV_W - 1), 0), (0, 0)))
    sf, sg = mixer_inputs(x_sample, mods, conv_in, bd, t)
    qb_s, k_s, v_s, kb_s, vb_s, small_s, cum_s = sf
    gq_s, gk_s, gv_s, gz_s, cst_s = sg
    hq = N_HEADS * t
    cs = cum_s[..., :N_HEADS]
    csn = jnp.pad(cs.reshape(bd, 1, hq), ((0, 0), (0, 0), (0, LANES - hq)))
    csq = cs.transpose(0, 2, 1).reshape(bd, hq, 1)
    q_rows = qb_s.reshape(bd, t, N_HEADS, HEAD_DIM).transpose(0, 2, 1, 3).reshape(bd, hq, HEAD_DIM)
    new_rows = lambda a: jnp.pad(a.reshape(bd, hq, HEAD_DIM), ((0, 0), (0, LANES - hq), (0, 0)))
    lf_rows = cache_logf[l].reshape(n_pool, 1, PAGE_SIZE * N_HEADS)
    o_rows = fox_sample_attention(q_rows, cache_k, cache_v, lf_rows, page_table, new_rows(kb_s), new_rows(vb_s),
                                  csn, csq, pp=8, t=t)
    o_fox_s = o_rows.reshape(bd, N_HEADS, t, HEAD_DIM).transpose(0, 2, 1, 3).reshape(bd, t, D_HEADS)
    padc = lambda a: jnp.pad(a, ((0, 0), (0, GDN_CHUNK - t), (0, 0)))
    cum_s_pad = jnp.pad(cum_s, ((0, 0), (0, GDN_CHUNK - t), (0, 0)), mode="edge")
    o_gdn_s, s_s = gdn_chunks(padc(gq_s), padc(gk_s), padc(gv_s), padc(gz_s), padc(small_s), cum_s_pad,
                              state_gdn[l].astype(F32), nw)
    x1_s, h2_s, lg_s = out_proj(x_sample, o_fox_s, o_gdn_s[:, :t], mods[:, :, 2], mods[:, :, 4], mods[:, :, 3],
                                rw2, w_top, w_bot, w_router_p, b_router_p, group=bd, tm=t)

    n_p, n_s = bp * s, bd * t
    h2_all = jnp.concatenate([h2_p.reshape(n_p, d), h2_s.reshape(n_s, d)], axis=0)
    logits = jnp.concatenate([lg_p.reshape(n_p, LANES), lg_s.reshape(n_s, LANES)], axis=0)[:, :N_EXPERTS]
    row_tok, gate, eb_start, eb_cnt, n_used, pos = _route(logits)
    x_sorted = moe_dispatch(h2_all, row_tok, n_used)
    hid = moe_gate_up(x_sorted, eb_start, eb_cnt, n_used, w_gu[l], b_gu[l].reshape(N_EXPERTS, 1, -1), tn=512)
    outg = moe_down(hid, eb_start, eb_cnt, n_used, w_down[l], b_down[l].reshape(N_EXPERTS, 1, -1), tn=1024)
    moe = sum(gate[:, k:k + 1] * outg[pos[:, k]] for k in range(TOP_K))
    y_p = x1_p + modp[:, :, 5] * moe[:n_p].reshape(bp, s, d)
    y_s = x1_s + mods[:, :, 5] * moe[n_p:].reshape(bd, t, d)

    hd = lambda a, b_, t_: a.reshape(1, b_, t_, N_HEADS, HEAD_DIM)
    return (y_p, y_s,
            hd(k_p, bp, s), hd(v_p, bp, s), small_p[..., :N_HEADS][None],
            s_p[None], cst_p[:, SUBLANES - (CONV_W - 1):][None],
            hd(k_s, bd, t), hd(v_s, bd, t), small_s[..., :N_HEADS][None],
            s_s[None], cst_s[:, SUBLANES - (CONV_W - 1):][None])


def kernel(x_prompt, x_sample, cache_k, cache_v, cache_logf, state_gdn, state_conv, page_table, c_prompt, c_sample, w_ada, b_ada, rms_mix, rms_ffn, w_in, b_forget, q_norm, k_norm, conv_w, a_log, dt_bias, gdn_norm, w_out, w_router, b_router, w_gu, b_gu, w_down, b_down):
    return _forward(x_prompt, x_sample, cache_k, cache_v, cache_logf, state_gdn, state_conv, page_table,
                    c_prompt, c_sample, w_ada, b_ada, rms_mix, rms_ffn, w_in, b_forget, q_norm, k_norm,
                    conv_w, a_log, dt_bias, gdn_norm, w_out, w_router, b_router, w_gu, b_gu, w_down, b_down)
```

```python
import functools

import jax
import jax.numpy as jnp
from jax import lax
from jax.experimental import pallas as pl
from jax.experimental.pallas import tpu as pltpu

F32 = jnp.float32
BF16 = jnp.bfloat16

D_MODEL = 2048
HEAD_DIM = 128
N_HEADS = 8
D_HEADS = N_HEADS * HEAD_DIM
CONV_W = 4
PAGE_SIZE = 128
N_EXPERTS = 32
TOP_K = 4
N_ADA = 6
RMS_EPS = 1e-6
L2_EPS = 1e-6
SWIGLU_ALPHA = 1.702
SWIGLU_LIMIT = 7.0

LANES = 128
SUBLANES = 8
GDN_CHUNK = 128
MOE_ROWS = 256
NEG_BIG = -1e30

_ARB = "arbitrary"


def _cparams(n_axes, vmem_mb):
    return pltpu.CompilerParams(dimension_semantics=(_ARB,) * n_axes,
                                vmem_limit_bytes=vmem_mb * 1024 * 1024)


def _dot(a, b):
    return jnp.dot(a, b, preferred_element_type=F32)


def _dot_nt(a, b):
    return lax.dot_general(a, b, (((1,), (1,)), ((), ())), preferred_element_type=F32)


def _dot_tn(a, b):
    return lax.dot_general(a, b, (((0,), (0,)), ((), ())), preferred_element_type=F32)


def _bmm(a, b):
    return lax.dot_general(a, b, (((2,), (1,)), ((0,), (0,))), preferred_element_type=F32)


def _bmm_nt(a, b):
    return lax.dot_general(a, b, (((2,), (2,)), ((0,), (0,))), preferred_element_type=F32)


def _softplus(z):
    return jnp.maximum(z, 0.0) + jnp.log(1.0 + jnp.exp(-jnp.abs(z)))


def _sigmoid(z):
    return 1.0 / (1.0 + jnp.exp(-z))


def _ada_kernel(c_ref, w_ref, b_ref, o_ref):
    c = c_ref[...]
    a = (c * _sigmoid(c)).astype(BF16)
    o_ref[...] = _dot(a, w_ref[...].astype(BF16)) + b_ref[...]


def ada_mod(c_all, w_ada, b_ada):
    m, d = c_all.shape
    n = w_ada.shape[1]
    tn = min(1024, n)
    return pl.pallas_call(
        _ada_kernel,
        grid=(n // tn,),
        in_specs=[pl.BlockSpec((m, d), lambda j: (0, 0)),
                  pl.BlockSpec((d, tn), lambda j: (0, j)),
                  pl.BlockSpec((1, tn), lambda j: (0, j))],
        out_specs=pl.BlockSpec((m, tn), lambda j: (0, j)),
        out_shape=jax.ShapeDtypeStruct((m, n), F32),
        compiler_params=_cparams(1, 40),
        name="ada_mod",
    )(c_all, w_ada, b_ada.reshape(1, n))


def _mod_norm_bf16(x_ref, sc_ref, sh_ref, rw_ref):
    x = x_ref[...]
    g, tm, d = x.shape
    ms = jnp.mean(x * x, axis=-1, keepdims=True)
    h = x * lax.rsqrt(ms + RMS_EPS) * rw_ref[...]
    h = h * (1.0 + sc_ref[...]) + sh_ref[...]
    return h.reshape(g * tm, d).astype(BF16)


def _per_head(p, fn):
    return jnp.concatenate([fn(p[:, h * HEAD_DIM:(h + 1) * HEAD_DIM]) for h in range(N_HEADS)], axis=-1)


def _rms_head(s):
    return s * lax.rsqrt(jnp.mean(s * s, axis=-1, keepdims=True) + RMS_EPS)


def _l2_head(s):
    return s * lax.rsqrt(jnp.sum(s * s, axis=-1, keepdims=True) + L2_EPS)


def _proj_fox_kernel(x_ref, sc_ref, sh_ref, rw_ref, w_ref, qn_ref, kn_ref, pv_ref,
                     q_ref, k_ref, v_ref, kb_ref, vb_ref, small_ref, cum_ref, carry_ref, *, seg):
    i = pl.program_id(1)
    g, tm, _ = x_ref.shape
    m = g * tm
    hb = _mod_norm_bf16(x_ref, sc_ref, sh_ref, rw_ref)

    qn = _per_head(_dot(hb, w_ref[:, 0:D_HEADS]), _rms_head) * qn_ref[...]
    q_ref[...] = (qn * HEAD_DIM ** -0.5).astype(BF16).reshape(g, tm, D_HEADS)
    kn = _per_head(_dot(hb, w_ref[:, D_HEADS:2 * D_HEADS]), _rms_head) * kn_ref[...]
    k_ref[...] = kn.reshape(g, tm, D_HEADS)
    kb_ref[...] = kn.astype(BF16).reshape(g, tm, D_HEADS)
    vv = _dot(hb, w_ref[:, 2 * D_HEADS:3 * D_HEADS])
    v_ref[...] = vv.reshape(g, tm, D_HEADS)
    vb_ref[...] = vv.astype(BF16).reshape(g, tm, D_HEADS)

    z = _dot(hb, w_ref[:, 3 * D_HEADS:3 * D_HEADS + LANES]) + pv_ref[0:1, :]
    lane = lax.broadcasted_iota(jnp.int32, (m, LANES), 1)
    sp = _softplus(z)
    logsig = z - sp
    gval = -jnp.exp(pv_ref[1:2, :]) * sp
    is_f = lane < N_HEADS
    is_g = (lane >= N_HEADS) & (lane < 2 * N_HEADS)
    is_b = (lane >= 2 * N_HEADS) & (lane < 3 * N_HEADS)
    small = jnp.where(is_f, logsig, jnp.where(is_g, gval, jnp.where(is_b, _sigmoid(z), 0.0)))
    small_ref[...] = small.reshape(g, tm, LANES)

    row = lax.broadcasted_iota(jnp.int32, (m, LANES), 0)
    pos = row & (tm - 1)
    rowmod = jnp.where(is_g, pos & (seg - 1), pos)
    c = small
    s = 1
    while s < tm:
        c = c + jnp.where(rowmod >= s, pltpu.roll(c, s, axis=0), 0.0)
        s *= 2
    lane3 = lax.broadcasted_iota(jnp.int32, (g, tm, LANES), 2)
    carry = jnp.where(i == 0, 0.0, carry_ref[...])
    c3 = c.reshape(g, tm, LANES) + jnp.where(lane3 < N_HEADS, carry, 0.0)
    cum_ref[...] = c3
    carry_ref[...] = c3[:, tm - 1:tm, :]


def proj_fox(x3, sc, sh, rms_w, w_fox, qn_t, kn_t, pvec, *, group, tm):
    nseq, t, d = x3.shape
    g = group
    seg = min(GDN_CHUNK, tm)
    grid = (nseq // g, t // tm)
    blk = lambda w: pl.BlockSpec((g, tm, w), lambda s, i: (s, i, 0))
    const2 = lambda a: pl.BlockSpec(a.shape, lambda s, i: (0, 0))
    mod = pl.BlockSpec((g, 1, d), lambda s, i: (s, 0, 0))
    outs = [jax.ShapeDtypeStruct((nseq, t, D_HEADS), BF16),
            jax.ShapeDtypeStruct((nseq, t, D_HEADS), F32),
            jax.ShapeDtypeStruct((nseq, t, D_HEADS), F32),
            jax.ShapeDtypeStruct((nseq, t, D_HEADS), BF16),
            jax.ShapeDtypeStruct((nseq, t, D_HEADS), BF16),
            jax.ShapeDtypeStruct((nseq, t, LANES), F32),
            jax.ShapeDtypeStruct((nseq, t, LANES), F32)]
    return pl.pallas_call(
        functools.partial(_proj_fox_kernel, seg=seg),
        grid=grid,
        in_specs=[blk(d), mod, mod, const2(rms_w),
                  pl.BlockSpec(w_fox.shape, lambda s, i: (0, 0), pipeline_mode=pl.Buffered(1)),
                  const2(qn_t), const2(kn_t), const2(pvec)],
        out_specs=[blk(D_HEADS)] * 5 + [blk(LANES)] * 2,
        out_shape=outs,
        scratch_shapes=[pltpu.VMEM((g, 1, LANES), F32)],
        compiler_params=_cparams(2, 48),
        name="proj_fox",
    )(x3, sc, sh, rms_w, w_fox, qn_t, kn_t, pvec)


def _proj_gdn_kernel(x_ref, sc_ref, sh_ref, rw_ref, w_ref, cw_ref, cin_ref,
                     q_ref, k_ref, v_ref, z_ref, cst_ref, cbuf_ref):
    i = pl.program_id(1)
    g, tm, _ = x_ref.shape
    hb = _mod_norm_bf16(x_ref, sc_ref, sh_ref, rw_ref)
    outs = (q_ref, k_ref, v_ref)
    for c in range(3):
        cols = slice(c * D_HEADS, (c + 1) * D_HEADS)
        pc = _dot(hb, w_ref[:, cols]).reshape(g, tm, D_HEADS)

        @pl.when(i == 0)
        def _():
            cbuf_ref[c, :, 0:SUBLANES, :] = cin_ref[:, :, cols]

        @pl.when(i > 0)
        def _():
            cbuf_ref[c, :, 0:SUBLANES, :] = cbuf_ref[c, :, tm:tm + SUBLANES, :]

        cbuf_ref[c, :, SUBLANES:tm + SUBLANES, :] = pc
        cst_ref[:, :, cols] = pc[:, tm - SUBLANES:tm, :]
        base = SUBLANES - (CONV_W - 1)
        y = None
        for j in range(CONV_W):
            term = cbuf_ref[c, :, base + j:base + j + tm, :] * cw_ref[j:j + 1, cols]
            y = term if y is None else y + term
        y = (y * _sigmoid(y)).reshape(g * tm, D_HEADS)
        if c == 0:
            y = _per_head(y, _l2_head) * HEAD_DIM ** -0.5
        elif c == 1:
            y = _per_head(y, _l2_head)
        outs[c][...] = y.reshape(g, tm, D_HEADS)
    z_ref[...] = _dot(hb, w_ref[:, 3 * D_HEADS:4 * D_HEADS]).reshape(g, tm, D_HEADS)


def proj_gdn(x3, sc, sh, rms_w, w_g, conv_w, conv_in, *, group, tm):
    nseq, t, d = x3.shape
    g = group
    grid = (nseq // g, t // tm)
    blk = lambda w: pl.BlockSpec((g, tm, w), lambda s, i: (s, i, 0))
    const2 = lambda a: pl.BlockSpec(a.shape, lambda s, i: (0, 0))
    mod = pl.BlockSpec((g, 1, d), lambda s, i: (s, 0, 0))
    st = pl.BlockSpec((g, SUBLANES, 3 * D_HEADS), lambda s, i: (s, 0, 0))
    outs = [jax.ShapeDtypeStruct((nseq, t, D_HEADS), F32)] * 4 + \
           [jax.ShapeDtypeStruct((nseq, SUBLANES, 3 * D_HEADS), F32)]
    return pl.pallas_call(
        _proj_gdn_kernel,
        grid=grid,
        in_specs=[blk(d), mod, mod, const2(rms_w),
                  pl.BlockSpec(w_g.shape, lambda s, i: (0, 0), pipeline_mode=pl.Buffered(1)),
                  const2(conv_w), st],
        out_specs=[blk(D_HEADS)] * 4 + [st],
        out_shape=outs,
        scratch_shapes=[pltpu.VMEM((3, g, tm + SUBLANES, D_HEADS), F32)],
        compiler_params=_cparams(2, 52),
        name="proj_gdn",
    )(x3, sc, sh, rms_w, w_g, conv_w, conv_in)


def _fox_prompt_kernel(q_ref, k_ref, v_ref, fc_ref, fr_ref, o_ref, *, tq):
    qi = pl.program_id(2)
    q = q_ref[0]
    fq = fc_ref[0]

    def step(j, carry, masked):
        m_i, l_i, acc = carry
        off = pl.multiple_of(j * tq, tq)
        ks = k_ref[0, pl.ds(off, tq), :]
        vs = v_ref[0, pl.ds(off, tq), :]
        s = _dot_nt(q, ks) + (fq - fr_ref[0, :, pl.ds(off, tq)])
        if masked:
            r = lax.broadcasted_iota(jnp.int32, (tq, tq), 0)
            c = lax.broadcasted_iota(jnp.int32, (tq, tq), 1)
            s = jnp.where(c <= r, s, NEG_BIG)
        m_new = jnp.maximum(m_i, jnp.max(s, axis=-1, keepdims=True))
        alpha = jnp.exp(m_i - m_new)
        p = jnp.exp(s - m_new)
        l_new = alpha * l_i + jnp.sum(p, axis=-1, keepdims=True)
        acc = alpha * acc + _dot(p.astype(BF16), vs)
        return m_new, l_new, acc

    init = (jnp.full((tq, 1), NEG_BIG, F32), jnp.zeros((tq, 1), F32), jnp.zeros((tq, HEAD_DIM), F32))
    carry = lax.fori_loop(0, qi, functools.partial(step, masked=False), init)
    _, l_i, acc = step(qi, carry, True)
    o_ref[0] = (acc / l_i).astype(BF16)


def fox_prompt_attention(qb, kb, vb, f_col, f_row, *, tq):
    b, s, _ = qb.shape
    grid = (b, N_HEADS, s // tq)
    return pl.pallas_call(
        functools.partial(_fox_prompt_kernel, tq=tq),
        grid=grid,
        in_specs=[pl.BlockSpec((1, tq, HEAD_DIM), lambda bi, h, qi: (bi, qi, h)),
                  pl.BlockSpec((1, s, HEAD_DIM), lambda bi, h, qi: (bi, 0, h)),
                  pl.BlockSpec((1, s, HEAD_DIM), lambda bi, h, qi: (bi, 0, h)),
                  pl.BlockSpec((1, tq, 1), lambda bi, h, qi: (bi * N_HEADS + h, qi, 0)),
                  pl.BlockSpec((1, 1, s), lambda bi, h, qi: (bi * N_HEADS + h, 0, 0))],
        out_specs=pl.BlockSpec((1, tq, HEAD_DIM), lambda bi, h, qi: (bi, qi, h)),
        out_shape=jax.ShapeDtypeStruct((b, s, D_HEADS), BF16),
        compiler_params=_cparams(3, 32),
        name="fox_prompt_attention",
    )(qb, kb, vb, f_col, f_row)


def _fox_sample_kernel(pt_ref, q_ref, *refs, pp, t):
    k_refs = refs[0:pp]
    v_refs = refs[pp:2 * pp]
    lf_refs = refs[2 * pp:3 * pp]
    kn_ref, vn_ref, csn_ref, csq_ref, o_ref, m_ref, l_ref, acc_ref, carry_ref = refs[3 * pp:]
    s = pl.program_id(1)
    ns = pl.num_programs(1)
    hq = N_HEADS * t
    ncol = PAGE_SIZE * N_HEADS

    @pl.when(s == 0)
    def _():
        m_ref[...] = jnp.full(m_ref.shape, NEG_BIG, F32)
        l_ref[...] = jnp.zeros(l_ref.shape, F32)
        acc_ref[...] = jnp.zeros(acc_ref.shape, F32)
        carry_ref[...] = jnp.zeros(carry_ref.shape, F32)

    q = q_ref[0]
    csq = csq_ref[0]

    def update(sc, vals, width):
        m_old = m_ref[...]
        m_new = jnp.maximum(m_old, jnp.max(sc, axis=-1, keepdims=True))
        alpha = jnp.exp(m_old - m_new)
        p = jnp.exp(sc - m_new)
        l_ref[...] = alpha * l_ref[...] + jnp.sum(p, axis=-1, keepdims=True)
        pv = None
        for j, vv in enumerate(vals):
            term = _dot(p[:, j * width:(j + 1) * width].astype(BF16), vv)
            pv = term if pv is None else pv + term
        acc_ref[...] = alpha * acc_ref[...] + pv
        m_ref[...] = m_new

    lane = lax.broadcasted_iota(jnp.int32, (1, ncol), 1)
    carry = carry_ref[...]
    bias = [None] * pp
    for j in reversed(range(pp)):
        lf = lf_refs[j][...]
        c = lf
        step = N_HEADS
        while step < ncol:
            c = c + jnp.where(lane + step < ncol, pltpu.roll(c, ncol - step, axis=1), 0.0)
            step *= 2
        bias[j] = c - lf + carry
        tot = jnp.where(lane < N_HEADS, c, 0.0)
        step = N_HEADS
        while step < ncol:
            tot = tot + pltpu.roll(tot, step, axis=1)
            step *= 2
        carry = carry + tot
    carry_ref[...] = carry

    row = lax.broadcasted_iota(jnp.int32, (hq, ncol), 0)
    col = lax.broadcasted_iota(jnp.int32, (hq, ncol), 1)
    t_shift = t.bit_length() - 1
    h_shift = N_HEADS.bit_length() - 1
    same_head = (col & (N_HEADS - 1)) == (row >> t_shift)
    sc = [jnp.where(same_head, _dot_nt(q, k_refs[j][...].reshape(ncol, HEAD_DIM).astype(BF16)) + (csq + bias[j]),
                    NEG_BIG) for j in range(pp)]
    update(jnp.concatenate(sc, axis=1),
           [v_refs[j][...].reshape(ncol, HEAD_DIM).astype(BF16) for j in range(pp)], ncol)

    @pl.when(s == ns - 1)
    def _():
        rn = lax.broadcasted_iota(jnp.int32, (hq, LANES), 0)
        cn = lax.broadcasted_iota(jnp.int32, (hq, LANES), 1)
        ok = ((cn & (N_HEADS - 1)) == (rn >> t_shift)) & ((cn >> h_shift) <= (rn & (t - 1)))
        sn = jnp.where(ok, _dot_nt(q, kn_ref[0]) + (csq - csn_ref[0]), NEG_BIG)
        update(sn, [vn_ref[0]], LANES)
        o_ref[0] = (acc_ref[...] / l_ref[...]).astype(BF16)


def fox_sample_attention(q_rows, cache_k, cache_v, lf_rows, page_table, kn_pad, vn_pad, csn, csq, *, pp, t):
    bd, hq, _ = q_rows.shape
    n_pages = page_table.shape[1]
    ns = n_pages // pp
    ncol = PAGE_SIZE * N_HEADS
    page_of = lambda b, s, j, pt: pt[b, (ns - 1 - s) * pp + j]
    kv_spec = lambda j: pl.BlockSpec((None, None, PAGE_SIZE, N_HEADS, HEAD_DIM),
                                     lambda b, s, pt: (0, page_of(b, s, j, pt), 0, 0, 0))
    lf_spec = lambda j: pl.BlockSpec((None, 1, ncol), lambda b, s, pt: (page_of(b, s, j, pt), 0, 0))
    per_b = lambda shp: pl.BlockSpec((1,) + shp, lambda b, s, pt: (b, 0, 0))
    gs = pltpu.PrefetchScalarGridSpec(
        num_scalar_prefetch=1,
        grid=(bd, ns),
        in_specs=[per_b((hq, HEAD_DIM))] + [kv_spec(j) for j in range(pp)] * 2 + [lf_spec(j) for j in range(pp)] +
                 [per_b((LANES, HEAD_DIM)), per_b((LANES, HEAD_DIM)), per_b((1, LANES)), per_b((hq, 1))],
        out_specs=per_b((hq, HEAD_DIM)),
        scratch_shapes=[pltpu.VMEM((hq, 1), F32), pltpu.VMEM((hq, 1), F32), pltpu.VMEM((hq, HEAD_DIM), F32),
                        pltpu.VMEM((1, ncol), F32)])
    return pl.pallas_call(
        functools.partial(_fox_sample_kernel, pp=pp, t=t), grid_spec=gs,
        out_shape=jax.ShapeDtypeStruct((bd, hq, HEAD_DIM), BF16),
        compiler_params=_cparams(2, 48),
        name="fox_sample_attention",
    )(page_table, q_rows, *([cache_k] * pp), *([cache_v] * pp), *([lf_rows] * pp), kn_pad, vn_pad, csn, csq)


def _gdn_kernel(q_ref, k_ref, v_ref, z_ref, small_ref, cum_ref, s0_ref, nw_ref, o_ref, so_ref, s_ref):
    ci = pl.program_id(1)
    c = GDN_CHUNK

    @pl.when(ci == 0)
    def _():
        s_ref[...] = s0_ref[0]

    cum = cum_ref[0]
    cum_t = cum.T
    small = small_ref[0]
    r = lax.broadcasted_iota(jnp.int32, (c, c), 0)
    cc = lax.broadcasted_iota(jnp.int32, (c, c), 1)
    tri = cc <= r
    strict = cc < r
    eye = (cc == r).astype(F32)
    pair_masks = []
    shift = 0
    while (1 << shift) < c:
        rb = r >> shift
        pair_masks.append(((rb & 1) == 1) & ((cc >> shift) == rb - 1))
        shift += 1
    heads = range(N_HEADS)
    stack = lambda f: jnp.stack([f(h) for h in heads], axis=0)
    hs = lambda h: slice(h * HEAD_DIM, (h + 1) * HEAD_DIM)
    q = stack(lambda h: q_ref[0, :, hs(h)])
    k = stack(lambda h: k_ref[0, :, hs(h)])
    v = stack(lambda h: v_ref[0, :, hs(h)])
    gc = stack(lambda h: cum[:, N_HEADS + h:N_HEADS + h + 1])
    gr = stack(lambda h: cum_t[N_HEADS + h:N_HEADS + h + 1, :])
    bt = stack(lambda h: small[:, 2 * N_HEADS + h:2 * N_HEADS + h + 1])
    g_last = gc[:, c - 1:c, :]
    decay = jnp.where(tri, jnp.exp(jnp.where(tri, gc - gr, 0.0)), 0.0)
    kb = k * bt
    kk_qk = _bmm_nt(jnp.concatenate([kb, q], axis=1).astype(BF16), k.astype(BF16))
    lmat = jnp.where(strict, kk_qk[:, :c] * decay, 0.0)
    a_in = jnp.where(tri, kk_qk[:, c:] * decay, 0.0)
    tm = jnp.broadcast_to(eye, (N_HEADS, c, c))
    for e_mask in pair_masks:
        tb = tm.astype(BF16)
        te = _bmm(tb, jnp.where(e_mask, lmat, 0.0).astype(BF16))
        tm = tm - _bmm(te.astype(BF16), tb)
    eg = jnp.exp(gc)
    uwb = _bmm(tm.astype(BF16), jnp.concatenate([v * bt, kb * eg], axis=2).astype(BF16)).astype(BF16)
    au = _bmm(a_in.astype(BF16), uwb)
    kt = k * jnp.exp(g_last - gc)
    kt_t = stack(lambda h: kt[h].T)
    ku = _bmm(kt_t.astype(BF16), uwb)
    qt = q * eg - au[:, :, HEAD_DIM:]
    s_all = s_ref[...]
    x = _bmm(jnp.concatenate([-ku[:, :, HEAD_DIM:], qt], axis=1).astype(BF16), s_all.astype(BF16))
    s_new = s_all * jnp.exp(g_last) + x[:, :HEAD_DIM] + ku[:, :, :HEAD_DIM]
    s_ref[...] = s_new
    so_ref[0] = s_new
    o = x[:, HEAD_DIM:] + au[:, :, :HEAD_DIM]
    o = o * lax.rsqrt(jnp.mean(o * o, axis=-1, keepdims=True) + RMS_EPS) * nw_ref[...]
    for h in heads:
        zh = z_ref[0, :, hs(h)]
        o_ref[0, :, hs(h)] = (o[h] * (zh * _sigmoid(zh))).astype(BF16)


def gdn_chunks(q, k, v, z, small, cum, s0, norm_w):
    nseq, tp, _ = q.shape
    c = GDN_CHUNK
    blk = lambda w: pl.BlockSpec((1, c, w), lambda s, i: (s, i, 0))
    st = pl.BlockSpec((1, N_HEADS, HEAD_DIM, HEAD_DIM), lambda s, i: (s, 0, 0, 0))
    return pl.pallas_call(
        _gdn_kernel,
        grid=(nseq, tp // c),
        in_specs=[blk(D_HEADS)] * 4 + [blk(LANES)] * 2 + [st, pl.BlockSpec((1, HEAD_DIM), lambda s, i: (0, 0))],
        out_specs=[blk(D_HEADS), st],
        out_shape=[jax.ShapeDtypeStruct((nseq, tp, D_HEADS), BF16),
                   jax.ShapeDtypeStruct((nseq, N_HEADS, HEAD_DIM, HEAD_DIM), F32)],
        scratch_shapes=[pltpu.VMEM((N_HEADS, HEAD_DIM, HEAD_DIM), F32)],
        compiler_params=_cparams(2, 32),
        name="gdn_chunks",
    )(q, k, v, z, small, cum, s0, norm_w)


def _split3(a):
    hi = a.astype(BF16)
    lo = (a - hi.astype(F32)).astype(BF16)
    return hi, lo


def _out_proj_kernel(x_ref, of_ref, og_ref, g1_ref, sc_ref, sh_ref, rw_ref, wt_ref, wb_ref, wr_ref, br_ref,
                     x1_ref, h2_ref, lg_ref):
    g, tm, d = x_ref.shape
    m = g * tm
    mix = _dot(of_ref[...].reshape(m, D_HEADS), wt_ref[...]) + _dot(og_ref[...].reshape(m, D_HEADS), wb_ref[...])
    x1 = x_ref[...] + g1_ref[...] * mix.reshape(g, tm, d)
    x1_ref[...] = x1
    ms = jnp.mean(x1 * x1, axis=-1, keepdims=True)
    h2 = x1 * lax.rsqrt(ms + RMS_EPS) * rw_ref[...]
    h2 = (h2 * (1.0 + sc_ref[...]) + sh_ref[...]).reshape(m, d)
    h2_ref[...] = h2.reshape(g, tm, d)
    hh, hl = _split3(h2)
    wh, wl = _split3(wr_ref[...])
    lg = _dot(hh, wh) + (_dot(hh, wl) + _dot(hl, wh)) + br_ref[...]
    lg_ref[...] = lg.reshape(g, tm, LANES)


def out_proj(x3, o_fox, o_gdn, g1, sc2, sh2, rms_w, w_top, w_bot, w_router_p, b_router_p, *, group, tm):
    nseq, t, d = x3.shape
    g = group
    blk = lambda w: pl.BlockSpec((g, tm, w), lambda s, i: (s, i, 0))
    const2 = lambda a: pl.BlockSpec(a.shape, lambda s, i: (0, 0))
    res = lambda a: pl.BlockSpec(a.shape, lambda s, i: (0, 0), pipeline_mode=pl.Buffered(1))
    mod = pl.BlockSpec((g, 1, d), lambda s, i: (s, 0, 0))
    return pl.pallas_call(
        _out_proj_kernel,
        grid=(nseq // g, t // tm),
        in_specs=[blk(d), blk(D_HEADS), blk(D_HEADS), mod, mod, mod, const2(rms_w),
                  res(w_top), res(w_bot), const2(w_router_p), const2(b_router_p)],
        out_specs=[blk(d), blk(d), blk(LANES)],
        out_shape=[jax.ShapeDtypeStruct((nseq, t, d), F32), jax.ShapeDtypeStruct((nseq, t, d), F32),
                   jax.ShapeDtypeStruct((nseq, t, LANES), F32)],
        compiler_params=_cparams(2, 40),
        name="out_proj",
    )(x3, o_fox, o_gdn, g1, sc2, sh2, rms_w, w_top, w_bot, w_router_p, b_router_p)


def _row_copy(src_hbm, buf_ref, sem, tok, r):
    return pltpu.make_async_copy(src_hbm.at[pl.ds(tok, 1)], buf_ref.at[pl.ds(r, 1)], sem)


def _moe_dispatch_kernel(tok_ref, nu_ref, src_hbm, o_ref, buf_ref, sem):
    i = pl.program_id(0)
    rows = buf_ref.shape[0]

    @pl.when(i < nu_ref[0])
    def _():
        base = i * rows

        def issue(g, carry):
            for u in range(SUBLANES):
                r = g * SUBLANES + u
                _row_copy(src_hbm, buf_ref, sem, tok_ref[base + r], r).start(priority=u % 2)
            return carry

        lax.fori_loop(0, rows // SUBLANES, issue, 0)
        pltpu.make_async_copy(src_hbm.at[pl.ds(0, rows)], buf_ref, sem).wait()
        o_ref[...] = buf_ref[...].astype(BF16)

    @pl.when(i >= nu_ref[0])
    def _():
        o_ref[...] = jnp.zeros(o_ref.shape, o_ref.dtype)


def moe_dispatch(h2_all, row_tok, n_used):
    n, d = h2_all.shape
    r = row_tok.shape[0]
    gs = pltpu.PrefetchScalarGridSpec(
        num_scalar_prefetch=2,
        grid=(r // MOE_ROWS,),
        in_specs=[pl.BlockSpec(memory_space=pl.ANY)],
        out_specs=pl.BlockSpec((MOE_ROWS, d), lambda i, tok, nu: (i, 0)),
        scratch_shapes=[pltpu.VMEM((MOE_ROWS, d), F32), pltpu.SemaphoreType.DMA(())])
    return pl.pallas_call(
        _moe_dispatch_kernel, grid_spec=gs,
        out_shape=jax.ShapeDtypeStruct((r, d), BF16),
        compiler_params=_cparams(1, 16),
        name="moe_dispatch",
    )(row_tok, n_used, h2_all)


def _run_weights(i, col_offsets, meta, w_hbm, wf_ref, wb_refs, sem):
    be_ref, first_ref, par_ref, nxt_ref = meta
    tn = wf_ref.shape[3]

    def copies(e, slot):
        return [pltpu.make_async_copy(w_hbm.at[e, :, pl.ds(pl.multiple_of(c0, tn), tn)], wf_ref.at[slot, k],
                                      sem.at[slot]) for k, c0 in enumerate(col_offsets)]

    @pl.when(first_ref[i] == 1)
    def _():
        slot = par_ref[i]

        @pl.when(i == 0)
        def _():
            for cp in copies(be_ref[i], slot):
                cp.start()

        for cp in copies(be_ref[i], slot):
            cp.wait()
        for k, wb_ref in enumerate(wb_refs):
            wb_ref[...] = wf_ref[slot, k].astype(BF16)

        @pl.when(nxt_ref[i] >= 0)
        def _():
            for cp in copies(nxt_ref[i], 1 - slot):
                cp.start()


def _moe_gu_kernel(be_ref, first_ref, par_ref, nxt_ref, nu_ref, x_ref, w_hbm, bg_ref, bl_ref, h_ref,
                   wf_ref, wgb_ref, wlb_ref, sem):
    j = pl.program_id(0)
    i = pl.program_id(1)
    tn = h_ref.shape[1]
    nj = pl.num_programs(0)

    @pl.when(i < nu_ref[0])
    def _():
        _run_weights(i, (j * tn, (nj + j) * tn), (be_ref, first_ref, par_ref, nxt_ref), w_hbm, wf_ref,
                     (wgb_ref, wlb_ref), sem)
        x = x_ref[...]
        glu = jnp.minimum(_dot(x, wgb_ref[...]) + bg_ref[0], SWIGLU_LIMIT)
        lin = jnp.clip(_dot(x, wlb_ref[...]) + bl_ref[0], -SWIGLU_LIMIT, SWIGLU_LIMIT)
        h_ref[...] = (glu * _sigmoid(SWIGLU_ALPHA * glu) * (lin + 1.0)).astype(BF16)

    @pl.when(i >= nu_ref[0])
    def _():
        h_ref[...] = jnp.zeros(h_ref.shape, h_ref.dtype)


def _moe_specs(rows, d_in, tn, n_bias, nj):
    blk_i = lambda i, nu: jnp.minimum(i, nu[0] - 1)
    x_spec = pl.BlockSpec((rows, d_in), lambda j, i, be, fi, pa, nx, nu: (blk_i(i, nu), 0))
    bias = [pl.BlockSpec((1, 1, tn), (lambda j, i, be, fi, pa, nx, nu, k=k: (be[blk_i(i, nu)], 0, k * nj + j)))
            for k in range(n_bias)]
    out_spec = pl.BlockSpec((rows, tn), lambda j, i, be, fi, pa, nx, nu: (i, j))
    return x_spec, bias, out_spec


def moe_gate_up(x_sorted, meta, n_used, w_gu, b_gu3, *, tn):
    r, d = x_sorted.shape
    d_ff = w_gu.shape[2] // 2
    nj = d_ff // tn
    x_spec, bias, out_spec = _moe_specs(MOE_ROWS, d, tn, 2, nj)
    gs = pltpu.PrefetchScalarGridSpec(
        num_scalar_prefetch=5,
        grid=(nj, r // MOE_ROWS),
        in_specs=[x_spec, pl.BlockSpec(memory_space=pl.ANY)] + bias,
        out_specs=out_spec,
        scratch_shapes=[pltpu.VMEM((2, 2, d, tn), F32), pltpu.VMEM((d, tn), BF16), pltpu.VMEM((d, tn), BF16),
                        pltpu.SemaphoreType.DMA((2,))])
    return pl.pallas_call(
        _moe_gu_kernel, grid_spec=gs,
        out_shape=jax.ShapeDtypeStruct((r, d_ff), BF16),
        compiler_params=_cparams(2, 52),
        name="moe_gate_up",
    )(*meta, n_used, x_sorted, w_gu, b_gu3, b_gu3)


def _moe_down_kernel(be_ref, first_ref, par_ref, nxt_ref, nu_ref, h_ref, w_hbm, b_ref, o_ref, wf_ref, wb_ref, sem):
    j = pl.program_id(0)
    i = pl.program_id(1)
    tn = o_ref.shape[1]

    @pl.when(i < nu_ref[0])
    def _():
        _run_weights(i, (j * tn,), (be_ref, first_ref, par_ref, nxt_ref), w_hbm, wf_ref, (wb_ref,), sem)
        o_ref[...] = _dot(h_ref[...], wb_ref[...]) + b_ref[0]

    @pl.when(i >= nu_ref[0])
    def _():
        o_ref[...] = jnp.zeros(o_ref.shape, o_ref.dtype)


def moe_down(h_sorted, meta, n_used, w_down, b_down3, *, tn):
    r, d_ff = h_sorted.shape
    d = w_down.shape[2]
    x_spec, bias, out_spec = _moe_specs(MOE_ROWS, d_ff, tn, 1, d // tn)
    gs = pltpu.PrefetchScalarGridSpec(
        num_scalar_prefetch=5,
        grid=(d // tn, r // MOE_ROWS),
        in_specs=[x_spec, pl.BlockSpec(memory_space=pl.ANY)] + bias,
        out_specs=out_spec,
        scratch_shapes=[pltpu.VMEM((2, 1, d_ff, tn), F32), pltpu.VMEM((d_ff, tn), BF16),
                        pltpu.SemaphoreType.DMA((2,))])
    return pl.pallas_call(
        _moe_down_kernel, grid_spec=gs,
        out_shape=jax.ShapeDtypeStruct((r, d), F32),
        compiler_params=_cparams(2, 40),
        name="moe_down",
    )(*meta, n_used, h_sorted, w_down, b_down3)


def _route(logits):
    n = logits.shape[0]
    a = n * TOP_K
    top_val, top_idx = lax.top_k(logits, TOP_K)
    gate = jax.nn.softmax(top_val, axis=-1)
    flat_e = top_idx.reshape(a).astype(jnp.int32)
    experts = jnp.arange(N_EXPERTS, dtype=jnp.int32)
    order = jnp.argsort(flat_e).astype(jnp.int32)
    counts = jnp.sum((flat_e[:, None] == experts[None, :]).astype(jnp.int32), axis=0)
    starts = jnp.cumsum(counts) - counts
    padded = -(-counts // MOE_ROWS) * MOE_ROWS
    pad_ends = jnp.cumsum(padded)
    pad_starts = pad_ends - padded
    sorted_e = flat_e[order]
    dest = pad_starts[sorted_e] + jnp.arange(a, dtype=jnp.int32) - starts[sorted_e]
    pos = dest[jnp.argsort(order)].reshape(n, TOP_K)
    nb = -(-(a + N_EXPERTS * (MOE_ROWS - 1)) // MOE_ROWS)
    rows = jnp.arange(nb * MOE_ROWS, dtype=jnp.int32)
    row_e = jnp.minimum(jnp.sum((pad_ends[None, :] <= rows[:, None]).astype(jnp.int32), axis=1), N_EXPERTS - 1)
    off = rows - pad_starts[row_e]
    src = jnp.clip(starts[row_e] + off, 0, a - 1)
    row_tok = jnp.where(off < counts[row_e], order[src] // TOP_K, 0).astype(jnp.int32)
    n_used = pad_ends[-1] // MOE_ROWS
    blocks = jnp.arange(nb, dtype=jnp.int32)
    block_e = row_e[::MOE_ROWS]
    used = blocks < n_used
    first = used & ((blocks == 0) | (block_e != jnp.roll(block_e, 1)))
    parity = (jnp.cumsum(first.astype(jnp.int32)) - 1) & 1
    later_first = first[None, :] & (blocks[None, :] > blocks[:, None])
    next_first = jnp.min(jnp.where(later_first, blocks[None, :], nb), axis=1)
    next_e = jnp.where(next_first < nb, block_e[jnp.minimum(next_first, nb - 1)], -1)
    meta = (block_e.astype(jnp.int32), first.astype(jnp.int32), parity.astype(jnp.int32), next_e.astype(jnp.int32))
    return row_tok, gate, meta, n_used.astype(jnp.int32).reshape(1), pos


def _forward(x_prompt, x_sample, cache_k, cache_v, cache_logf, state_gdn, state_conv, page_table,
             c_prompt, c_sample, w_ada, b_ada, rms_mix, rms_ffn, w_in, b_forget, q_norm, k_norm,
             conv_w, a_log, dt_bias, gdn_norm, w_out, w_router, b_router, w_gu, b_gu, w_down, b_down):
    depth = w_ada.shape[0]
    assert depth == 1, "single-layer trunk"
    l = 0
    bp, s, d = x_prompt.shape
    bd, t, _ = x_sample.shape
    n_pool = cache_k.shape[1]
    tm_p = 256

    n_c = bp + bd
    c_all = jnp.concatenate([c_prompt, c_sample, jnp.zeros((-n_c % SUBLANES, d), F32)], axis=0)
    mod = ada_mod(c_all, w_ada[l], b_ada[l])
    modp = mod[:bp].reshape(bp, 1, N_ADA, d)
    mods = mod[bp:n_c].reshape(bd, 1, N_ADA, d)

    w = w_in[l]
    o_ff, o_g, o_ab = 3 * D_HEADS, 3 * D_HEADS + N_HEADS, 7 * D_HEADS + N_HEADS
    w_small = jnp.concatenate([w[:, o_ff:o_g], w[:, o_ab:o_ab + 2 * N_HEADS]], axis=1)
    w_small = jnp.pad(w_small, ((0, 0), (0, LANES - 3 * N_HEADS)))
    w_fox = jnp.concatenate([w[:, :o_ff], w_small], axis=1).astype(BF16)
    w_g = w[:, o_g:o_ab].astype(BF16)
    qn_t = jnp.tile(q_norm[l], N_HEADS).reshape(1, D_HEADS)
    kn_t = jnp.tile(k_norm[l], N_HEADS).reshape(1, D_HEADS)
    pvec = jnp.zeros((SUBLANES, LANES), F32)
    pvec = pvec.at[0, 0:N_HEADS].set(b_forget[l]).at[0, N_HEADS:2 * N_HEADS].set(dt_bias[l])
    pvec = pvec.at[1, N_HEADS:2 * N_HEADS].set(a_log[l])
    rw1 = rms_mix[l].reshape(1, d)
    rw2 = rms_ffn[l].reshape(1, d)
    w_top = w_out[l][:D_HEADS].astype(BF16)
    w_bot = w_out[l][D_HEADS:].astype(BF16)
    w_router_p = jnp.pad(w_router[l], ((0, 0), (0, LANES - N_EXPERTS)))
    b_router_p = jnp.pad(b_router[l], (0, LANES - N_EXPERTS)).reshape(1, LANES)
    nw = gdn_norm[l].reshape(1, HEAD_DIM)

    def mixer_inputs(x3, m4, conv_in, group, tm):
        sh1, sc1 = m4[:, :, 0], m4[:, :, 1]
        pf = proj_fox(x3, sc1, sh1, rw1, w_fox, qn_t, kn_t, pvec, group=group, tm=tm)
        pg = proj_gdn(x3, sc1, sh1, rw1, w_g, conv_w[l], conv_in, group=group, tm=tm)
        return pf, pg

    pf, pg = mixer_inputs(x_prompt, modp, jnp.zeros((bp, SUBLANES, 3 * D_HEADS), F32), 1, tm_p)
    qb_p, k_p, v_p, kb_p, vb_p, small_p, cum_p = pf
    gq_p, gk_p, gv_p, gz_p, cst_p = pg
    f_t = cum_p[..., :N_HEADS].transpose(0, 2, 1).reshape(bp * N_HEADS, s)
    o_fox_p = fox_prompt_attention(qb_p, kb_p, vb_p, f_t.reshape(bp * N_HEADS, s, 1),
                                   f_t.reshape(bp * N_HEADS, 1, s), tq=512)
    o_gdn_p, s_p = gdn_chunks(gq_p, gk_p, gv_p, gz_p, small_p, cum_p,
                              jnp.zeros((bp, N_HEADS, HEAD_DIM, HEAD_DIM), F32), nw)
    x1_p, h2_p, lg_p = out_proj(x_prompt, o_fox_p, o_gdn_p, modp[:, :, 2], modp[:, :, 4], modp[:, :, 3],
                                rw2, w_top, w_bot, w_router_p, b_router_p, group=1, tm=tm_p)

    conv_in = jnp.pad(state_conv[l], ((0, 0), (SUBLANES - (CONV_W - 1), 0), (0, 0)))
    sf, sg = mixer_inputs(x_sample, mods, conv_in, bd, t)
    qb_s, k_s, v_s, kb_s, vb_s, small_s, cum_s = sf
    gq_s, gk_s, gv_s, gz_s, cst_s = sg
    hq = N_HEADS * t
    cs = cum_s[..., :N_HEADS]
    csn = jnp.pad(cs.reshape(bd, 1, hq), ((0, 0), (0, 0), (0, LANES - hq)))
    csq = cs.transpose(0, 2, 1).reshape(bd, hq, 1)
    q_rows = qb_s.reshape(bd, t, N_HEADS, HEAD_DIM).transpose(0, 2, 1, 3).reshape(bd, hq, HEAD_DIM)
    new_rows = lambda a: jnp.pad(a.reshape(bd, hq, HEAD_DIM), ((0, 0), (0, LANES - hq), (0, 0)))
    lf_rows = cache_logf[l].reshape(n_pool, 1, PAGE_SIZE * N_HEADS)
    o_rows = fox_sample_attention(q_rows, cache_k, cache_v, lf_rows, page_table, new_rows(kb_s), new_rows(vb_s),
                                  csn, csq, pp=8, t=t)
    o_fox_s = o_rows.reshape(bd, N_HEADS, t, HEAD_DIM).transpose(0, 2, 1, 3).reshape(bd, t, D_HEADS)
    padc = lambda a: jnp.pad(a, ((0, 0), (0, GDN_CHUNK - t), (0, 0)))
    cum_s_pad = jnp.pad(cum_s, ((0, 0), (0, GDN_CHUNK - t), (0, 0)), mode="edge")
    o_gdn_s, s_s = gdn_chunks(padc(gq_s), padc(gk_s), padc(gv_s), padc(gz_s), padc(small_s), cum_s_pad,
                              state_gdn[l].astype(F32), nw)
    x1_s, h2_s, lg_s = out_proj(x_sample, o_fox_s, o_gdn_s[:, :t], mods[:, :, 2], mods[:, :, 4], mods[:, :, 3],
                                rw2, w_top, w_bot, w_router_p, b_router_p, group=bd, tm=t)

    n_p, n_s = bp * s, bd * t
    h2_all = jnp.concatenate([h2_p.reshape(n_p, d), h2_s.reshape(n_s, d)], axis=0)
    logits = jnp.concatenate([lg_p.reshape(n_p, LANES), lg_s.reshape(n_s, LANES)], axis=0)[:, :N_EXPERTS]
    row_tok, gate, meta, n_used, pos = _route(logits)
    x_sorted = moe_dispatch(h2_all, row_tok, n_used)
    hid = moe_gate_up(x_sorted, meta, n_used, w_gu[l], b_gu[l].reshape(N_EXPERTS, 1, -1), tn=512)
    outg = moe_down(hid, meta, n_used, w_down[l], b_down[l].reshape(N_EXPERTS, 1, -1), tn=1024)
    moe = sum(gate[:, k:k + 1] * outg[pos[:, k]] for k in range(TOP_K))
    y_p = x1_p + modp[:, :, 5] * moe[:n_p].reshape(bp, s, d)
    y_s = x1_s + mods[:, :, 5] * moe[n_p:].reshape(bd, t, d)

    hd = lambda a, b_, t_: a.reshape(1, b_, t_, N_HEADS, HEAD_DIM)
    return (y_p, y_s,
            hd(k_p, bp, s), hd(v_p, bp, s), small_p[..., :N_HEADS][None],
            s_p[None], cst_p[:, SUBLANES - (CONV_W - 1):][None],
            hd(k_s, bd, t), hd(v_s, bd, t), small_s[..., :N_HEADS][None],
            s_s[None], cst_s[:, SUBLANES - (CONV_W - 1):][None])


def kernel(x_prompt, x_sample, cache_k, cache_v, cache_logf, state_gdn, state_conv, page_table, c_prompt, c_sample, w_ada, b_ada, rms_mix, rms_ffn, w_in, b_forget, q_norm, k_norm, conv_w, a_log, dt_bias, gdn_norm, w_out, w_router, b_router, w_gu, b_gu, w_down, b_down):
    return _forward(x_prompt, x_sample, cache_k, cache_v, cache_logf, state_gdn, state_conv, page_table,
                    c_prompt, c_sample, w_ada, b_ada, rms_mix, rms_ffn, w_in, b_forget, q_norm, k_norm,
                    conv_w, a_log, dt_bias, gdn_norm, w_out, w_router, b_router, w_gu, b_gu, w_down, b_down)
```

```python
import functools

import jax
import jax.numpy as jnp
from jax import lax
from jax.experimental import pallas as pl
from jax.experimental.pallas import tpu as pltpu

F32 = jnp.float32
BF16 = jnp.bfloat16

D_MODEL = 2048
HEAD_DIM = 128
N_HEADS = 8
D_HEADS = N_HEADS * HEAD_DIM
CONV_W = 4
PAGE_SIZE = 128
N_EXPERTS = 32
TOP_K = 4
N_ADA = 6
RMS_EPS = 1e-6
L2_EPS = 1e-6
SWIGLU_ALPHA = 1.702
SWIGLU_LIMIT = 7.0

LANES = 128
SUBLANES = 8
GDN_CHUNK = 128
MOE_ROWS = 256
NEG_BIG = -1e30

_ARB = "arbitrary"


def _cparams(n_axes, vmem_mb):
    return pltpu.CompilerParams(dimension_semantics=(_ARB,) * n_axes,
                                vmem_limit_bytes=vmem_mb * 1024 * 1024)


def _dot(a, b):
    return jnp.dot(a, b, preferred_element_type=F32)


def _dot_nt(a, b):
    return lax.dot_general(a, b, (((1,), (1,)), ((), ())), preferred_element_type=F32)


def _dot_tn(a, b):
    return lax.dot_general(a, b, (((0,), (0,)), ((), ())), preferred_element_type=F32)


def _bmm(a, b):
    return lax.dot_general(a, b, (((2,), (1,)), ((0,), (0,))), preferred_element_type=F32)


def _bmm_nt(a, b):
    return lax.dot_general(a, b, (((2,), (2,)), ((0,), (0,))), preferred_element_type=F32)


def _softplus(z):
    return jnp.maximum(z, 0.0) + jnp.log(1.0 + jnp.exp(-jnp.abs(z)))


def _sigmoid(z):
    return 1.0 / (1.0 + jnp.exp(-z))


def _ada_kernel(c_ref, w_ref, b_ref, o_ref):
    c = c_ref[...]
    a = (c * _sigmoid(c)).astype(BF16)
    o_ref[...] = _dot(a, w_ref[...].astype(BF16)) + b_ref[...]


def ada_mod(c_all, w_ada, b_ada):
    m, d = c_all.shape
    n = w_ada.shape[1]
    tn = min(1024, n)
    return pl.pallas_call(
        _ada_kernel,
        grid=(n // tn,),
        in_specs=[pl.BlockSpec((m, d), lambda j: (0, 0)),
                  pl.BlockSpec((d, tn), lambda j: (0, j)),
                  pl.BlockSpec((1, tn), lambda j: (0, j))],
        out_specs=pl.BlockSpec((m, tn), lambda j: (0, j)),
        out_shape=jax.ShapeDtypeStruct((m, n), F32),
        compiler_params=_cparams(1, 40),
        name="ada_mod",
    )(c_all, w_ada, b_ada.reshape(1, n))


def _mod_norm_bf16(x_ref, sc_ref, sh_ref, rw_ref):
    x = x_ref[...]
    g, tm, d = x.shape
    ms = jnp.mean(x * x, axis=-1, keepdims=True)
    h = x * lax.rsqrt(ms + RMS_EPS) * rw_ref[...]
    h = h * (1.0 + sc_ref[...]) + sh_ref[...]
    return h.reshape(g * tm, d).astype(BF16)


def _per_head(p, fn):
    return jnp.concatenate([fn(p[:, h * HEAD_DIM:(h + 1) * HEAD_DIM]) for h in range(N_HEADS)], axis=-1)


def _rms_head(s):
    return s * lax.rsqrt(jnp.mean(s * s, axis=-1, keepdims=True) + RMS_EPS)


def _l2_head(s):
    return s * lax.rsqrt(jnp.sum(s * s, axis=-1, keepdims=True) + L2_EPS)


def _proj_fox_kernel(x_ref, sc_ref, sh_ref, rw_ref, w_ref, qn_ref, kn_ref, pv_ref,
                     q_ref, k_ref, v_ref, kb_ref, vb_ref, small_ref, cum_ref, carry_ref, *, seg):
    i = pl.program_id(1)
    g, tm, _ = x_ref.shape
    m = g * tm
    hb = _mod_norm_bf16(x_ref, sc_ref, sh_ref, rw_ref)

    qn = _per_head(_dot(hb, w_ref[:, 0:D_HEADS]), _rms_head) * qn_ref[...]
    q_ref[...] = (qn * HEAD_DIM ** -0.5).astype(BF16).reshape(g, tm, D_HEADS)
    kn = _per_head(_dot(hb, w_ref[:, D_HEADS:2 * D_HEADS]), _rms_head) * kn_ref[...]
    k_ref[...] = kn.reshape(g, tm, D_HEADS)
    kb_ref[...] = kn.astype(BF16).reshape(g, tm, D_HEADS)
    vv = _dot(hb, w_ref[:, 2 * D_HEADS:3 * D_HEADS])
    v_ref[...] = vv.reshape(g, tm, D_HEADS)
    vb_ref[...] = vv.astype(BF16).reshape(g, tm, D_HEADS)

    z = _dot(hb, w_ref[:, 3 * D_HEADS:3 * D_HEADS + LANES]) + pv_ref[0:1, :]
    lane = lax.broadcasted_iota(jnp.int32, (m, LANES), 1)
    sp = _softplus(z)
    logsig = z - sp
    gval = -jnp.exp(pv_ref[1:2, :]) * sp
    is_f = lane < N_HEADS
    is_g = (lane >= N_HEADS) & (lane < 2 * N_HEADS)
    is_b = (lane >= 2 * N_HEADS) & (lane < 3 * N_HEADS)
    small = jnp.where(is_f, logsig, jnp.where(is_g, gval, jnp.where(is_b, _sigmoid(z), 0.0)))
    small_ref[...] = small.reshape(g, tm, LANES)

    row = lax.broadcasted_iota(jnp.int32, (m, LANES), 0)
    pos = row & (tm - 1)
    rowmod = jnp.where(is_g, pos & (seg - 1), pos)
    c = small
    s = 1
    while s < tm:
        c = c + jnp.where(rowmod >= s, pltpu.roll(c, s, axis=0), 0.0)
        s *= 2
    lane3 = lax.broadcasted_iota(jnp.int32, (g, tm, LANES), 2)
    carry = jnp.where(i == 0, 0.0, carry_ref[...])
    c3 = c.reshape(g, tm, LANES) + jnp.where(lane3 < N_HEADS, carry, 0.0)
    cum_ref[...] = c3
    carry_ref[...] = c3[:, tm - 1:tm, :]


def proj_fox(x3, sc, sh, rms_w, w_fox, qn_t, kn_t, pvec, *, group, tm):
    nseq, t, d = x3.shape
    g = group
    seg = min(GDN_CHUNK, tm)
    grid = (nseq // g, t // tm)
    blk = lambda w: pl.BlockSpec((g, tm, w), lambda s, i: (s, i, 0))
    const2 = lambda a: pl.BlockSpec(a.shape, lambda s, i: (0, 0))
    mod = pl.BlockSpec((g, 1, d), lambda s, i: (s, 0, 0))
    outs = [jax.ShapeDtypeStruct((nseq, t, D_HEADS), BF16),
            jax.ShapeDtypeStruct((nseq, t, D_HEADS), F32),
            jax.ShapeDtypeStruct((nseq, t, D_HEADS), F32),
            jax.ShapeDtypeStruct((nseq, t, D_HEADS), BF16),
            jax.ShapeDtypeStruct((nseq, t, D_HEADS), BF16),
            jax.ShapeDtypeStruct((nseq, t, LANES), F32),
            jax.ShapeDtypeStruct((nseq, t, LANES), F32)]
    return pl.pallas_call(
        functools.partial(_proj_fox_kernel, seg=seg),
        grid=grid,
        in_specs=[blk(d), mod, mod, const2(rms_w),
                  pl.BlockSpec(w_fox.shape, lambda s, i: (0, 0), pipeline_mode=pl.Buffered(1)),
                  const2(qn_t), const2(kn_t), const2(pvec)],
        out_specs=[blk(D_HEADS)] * 5 + [blk(LANES)] * 2,
        out_shape=outs,
        scratch_shapes=[pltpu.VMEM((g, 1, LANES), F32)],
        compiler_params=_cparams(2, 48),
        name="proj_fox",
    )(x3, sc, sh, rms_w, w_fox, qn_t, kn_t, pvec)


def _proj_gdn_kernel(x_ref, sc_ref, sh_ref, rw_ref, w_ref, cw_ref, cin_ref,
                     q_ref, k_ref, v_ref, z_ref, cst_ref, cbuf_ref):
    i = pl.program_id(1)
    g, tm, _ = x_ref.shape
    hb = _mod_norm_bf16(x_ref, sc_ref, sh_ref, rw_ref)
    outs = (q_ref, k_ref, v_ref)
    for c in range(3):
        cols = slice(c * D_HEADS, (c + 1) * D_HEADS)
        pc = _dot(hb, w_ref[:, cols]).reshape(g, tm, D_HEADS)

        @pl.when(i == 0)
        def _():
            cbuf_ref[c, :, 0:SUBLANES, :] = cin_ref[:, :, cols]

        @pl.when(i > 0)
        def _():
            cbuf_ref[c, :, 0:SUBLANES, :] = cbuf_ref[c, :, tm:tm + SUBLANES, :]

        cbuf_ref[c, :, SUBLANES:tm + SUBLANES, :] = pc
        cst_ref[:, :, cols] = pc[:, tm - SUBLANES:tm, :]
        base = SUBLANES - (CONV_W - 1)
        y = None
        for j in range(CONV_W):
            term = cbuf_ref[c, :, base + j:base + j + tm, :] * cw_ref[j:j + 1, cols]
            y = term if y is None else y + term
        y = (y * _sigmoid(y)).reshape(g * tm, D_HEADS)
        if c == 0:
            y = _per_head(y, _l2_head) * HEAD_DIM ** -0.5
        elif c == 1:
            y = _per_head(y, _l2_head)
        outs[c][...] = y.reshape(g, tm, D_HEADS)
    z_ref[...] = _dot(hb, w_ref[:, 3 * D_HEADS:4 * D_HEADS]).reshape(g, tm, D_HEADS)


def proj_gdn(x3, sc, sh, rms_w, w_g, conv_w, conv_in, *, group, tm):
    nseq, t, d = x3.shape
    g = group
    grid = (nseq // g, t // tm)
    blk = lambda w: pl.BlockSpec((g, tm, w), lambda s, i: (s, i, 0))
    const2 = lambda a: pl.BlockSpec(a.shape, lambda s, i: (0, 0))
    mod = pl.BlockSpec((g, 1, d), lambda s, i: (s, 0, 0))
    st = pl.BlockSpec((g, SUBLANES, 3 * D_HEADS), lambda s, i: (s, 0, 0))
    outs = [jax.ShapeDtypeStruct((nseq, t, D_HEADS), F32)] * 4 + \
           [jax.ShapeDtypeStruct((nseq, SUBLANES, 3 * D_HEADS), F32)]
    return pl.pallas_call(
        _proj_gdn_kernel,
        grid=grid,
        in_specs=[blk(d), mod, mod, const2(rms_w),
                  pl.BlockSpec(w_g.shape, lambda s, i: (0, 0), pipeline_mode=pl.Buffered(1)),
                  const2(conv_w), st],
        out_specs=[blk(D_HEADS)] * 4 + [st],
        out_shape=outs,
        scratch_shapes=[pltpu.VMEM((3, g, tm + SUBLANES, D_HEADS), F32)],
        compiler_params=_cparams(2, 52),
        name="proj_gdn",
    )(x3, sc, sh, rms_w, w_g, conv_w, conv_in)


def _fox_prompt_kernel(q_ref, k_ref, v_ref, fc_ref, fr_ref, o_ref, *, tq):
    qi = pl.program_id(2)
    q = q_ref[0]
    fq = fc_ref[0]

    def step(j, carry, masked):
        m_i, l_i, acc = carry
        off = pl.multiple_of(j * tq, tq)
        ks = k_ref[0, pl.ds(off, tq), :]
        vs = v_ref[0, pl.ds(off, tq), :]
        s = _dot_nt(q, ks) + (fq - fr_ref[0, :, pl.ds(off, tq)])
        if masked:
            r = lax.broadcasted_iota(jnp.int32, (tq, tq), 0)
            c = lax.broadcasted_iota(jnp.int32, (tq, tq), 1)
            s = jnp.where(c <= r, s, NEG_BIG)
        m_new = jnp.maximum(m_i, jnp.max(s, axis=-1, keepdims=True))
        alpha = jnp.exp(m_i - m_new)
        p = jnp.exp(s - m_new)
        l_new = alpha * l_i + jnp.sum(p, axis=-1, keepdims=True)
        acc = alpha * acc + _dot(p.astype(BF16), vs)
        return m_new, l_new, acc

    init = (jnp.full((tq, 1), NEG_BIG, F32), jnp.zeros((tq, 1), F32), jnp.zeros((tq, HEAD_DIM), F32))
    carry = lax.fori_loop(0, qi, functools.partial(step, masked=False), init)
    _, l_i, acc = step(qi, carry, True)
    o_ref[0] = (acc / l_i).astype(BF16)


def fox_prompt_attention(qb, kb, vb, f_col, f_row, *, tq):
    b, s, _ = qb.shape
    grid = (b, N_HEADS, s // tq)
    return pl.pallas_call(
        functools.partial(_fox_prompt_kernel, tq=tq),
        grid=grid,
        in_specs=[pl.BlockSpec((1, tq, HEAD_DIM), lambda bi, h, qi: (bi, qi, h)),
                  pl.BlockSpec((1, s, HEAD_DIM), lambda bi, h, qi: (bi, 0, h)),
                  pl.BlockSpec((1, s, HEAD_DIM), lambda bi, h, qi: (bi, 0, h)),
                  pl.BlockSpec((1, tq, 1), lambda bi, h, qi: (bi * N_HEADS + h, qi, 0)),
                  pl.BlockSpec((1, 1, s), lambda bi, h, qi: (bi * N_HEADS + h, 0, 0))],
        out_specs=pl.BlockSpec((1, tq, HEAD_DIM), lambda bi, h, qi: (bi, qi, h)),
        out_shape=jax.ShapeDtypeStruct((b, s, D_HEADS), BF16),
        compiler_params=_cparams(3, 32),
        name="fox_prompt_attention",
    )(qb, kb, vb, f_col, f_row)


def _fox_sample_kernel(pt_ref, q_ref, *refs, pp, t):
    k_refs = refs[0:pp]
    v_refs = refs[pp:2 * pp]
    lf_refs = refs[2 * pp:3 * pp]
    kn_ref, vn_ref, csn_ref, csq_ref, o_ref, m_ref, l_ref, acc_ref, carry_ref = refs[3 * pp:]
    s = pl.program_id(1)
    ns = pl.num_programs(1)
    hq = N_HEADS * t
    ncol = PAGE_SIZE * N_HEADS

    @pl.when(s == 0)
    def _():
        m_ref[...] = jnp.full(m_ref.shape, NEG_BIG, F32)
        l_ref[...] = jnp.zeros(l_ref.shape, F32)
        acc_ref[...] = jnp.zeros(acc_ref.shape, F32)
        carry_ref[...] = jnp.zeros(carry_ref.shape, F32)

    q = q_ref[0]
    csq = csq_ref[0]

    def update(sc, vals, width):
        m_old = m_ref[...]
        m_new = jnp.maximum(m_old, jnp.max(sc, axis=-1, keepdims=True))
        alpha = jnp.exp(m_old - m_new)
        p = jnp.exp(sc - m_new)
        l_ref[...] = alpha * l_ref[...] + jnp.sum(p, axis=-1, keepdims=True)
        pv = None
        for j, vv in enumerate(vals):
            term = _dot(p[:, j * width:(j + 1) * width].astype(BF16), vv)
            pv = term if pv is None else pv + term
        acc_ref[...] = alpha * acc_ref[...] + pv
        m_ref[...] = m_new

    lane = lax.broadcasted_iota(jnp.int32, (1, ncol), 1)
    carry = carry_ref[...]
    bias = [None] * pp
    for j in reversed(range(pp)):
        lf = lf_refs[j][...]
        c = lf
        step = N_HEADS
        while step < ncol:
            c = c + jnp.where(lane + step < ncol, pltpu.roll(c, ncol - step, axis=1), 0.0)
            step *= 2
        bias[j] = c - lf + carry
        tot = jnp.where(lane < N_HEADS, c, 0.0)
        step = N_HEADS
        while step < ncol:
            tot = tot + pltpu.roll(tot, step, axis=1)
            step *= 2
        carry = carry + tot
    carry_ref[...] = carry

    row = lax.broadcasted_iota(jnp.int32, (hq, ncol), 0)
    col = lax.broadcasted_iota(jnp.int32, (hq, ncol), 1)
    t_shift = t.bit_length() - 1
    h_shift = N_HEADS.bit_length() - 1
    same_head = (col & (N_HEADS - 1)) == (row >> t_shift)
    sc = [jnp.where(same_head, _dot_nt(q, k_refs[j][...].reshape(ncol, HEAD_DIM).astype(BF16)) + (csq + bias[j]),
                    NEG_BIG) for j in range(pp)]
    update(jnp.concatenate(sc, axis=1),
           [v_refs[j][...].reshape(ncol, HEAD_DIM).astype(BF16) for j in range(pp)], ncol)

    @pl.when(s == ns - 1)
    def _():
        rn = lax.broadcasted_iota(jnp.int32, (hq, LANES), 0)
        cn = lax.broadcasted_iota(jnp.int32, (hq, LANES), 1)
        ok = ((cn & (N_HEADS - 1)) == (rn >> t_shift)) & ((cn >> h_shift) <= (rn & (t - 1)))
        sn = jnp.where(ok, _dot_nt(q, kn_ref[0]) + (csq - csn_ref[0]), NEG_BIG)
        update(sn, [vn_ref[0]], LANES)
        o_ref[0] = (acc_ref[...] / l_ref[...]).astype(BF16)


def fox_sample_attention(q_rows, cache_k, cache_v, lf_rows, page_table, kn_pad, vn_pad, csn, csq, *, pp, t):
    bd, hq, _ = q_rows.shape
    n_pages = page_table.shape[1]
    ns = n_pages // pp
    ncol = PAGE_SIZE * N_HEADS
    page_of = lambda b, s, j, pt: pt[b, (ns - 1 - s) * pp + j]
    kv_spec = lambda j: pl.BlockSpec((None, None, PAGE_SIZE, N_HEADS, HEAD_DIM),
                                     lambda b, s, pt: (0, page_of(b, s, j, pt), 0, 0, 0))
    lf_spec = lambda j: pl.BlockSpec((None, 1, ncol), lambda b, s, pt: (page_of(b, s, j, pt), 0, 0))
    per_b = lambda shp: pl.BlockSpec((1,) + shp, lambda b, s, pt: (b, 0, 0))
    gs = pltpu.PrefetchScalarGridSpec(
        num_scalar_prefetch=1,
        grid=(bd, ns),
        in_specs=[per_b((hq, HEAD_DIM))] + [kv_spec(j) for j in range(pp)] * 2 + [lf_spec(j) for j in range(pp)] +
                 [per_b((LANES, HEAD_DIM)), per_b((LANES, HEAD_DIM)), per_b((1, LANES)), per_b((hq, 1))],
        out_specs=per_b((hq, HEAD_DIM)),
        scratch_shapes=[pltpu.VMEM((hq, 1), F32), pltpu.VMEM((hq, 1), F32), pltpu.VMEM((hq, HEAD_DIM), F32),
                        pltpu.VMEM((1, ncol), F32)])
    return pl.pallas_call(
        functools.partial(_fox_sample_kernel, pp=pp, t=t), grid_spec=gs,
        out_shape=jax.ShapeDtypeStruct((bd, hq, HEAD_DIM), BF16),
        compiler_params=_cparams(2, 48),
        name="fox_sample_attention",
    )(page_table, q_rows, *([cache_k] * pp), *([cache_v] * pp), *([lf_rows] * pp), kn_pad, vn_pad, csn, csq)


def _gdn_kernel(q_ref, k_ref, v_ref, z_ref, small_ref, cum_ref, s0_ref, nw_ref, o_ref, so_ref, s_ref):
    ci = pl.program_id(1)
    c = GDN_CHUNK

    @pl.when(ci == 0)
    def _():
        s_ref[...] = s0_ref[0]

    cum = cum_ref[0]
    cum_t = cum.T
    small = small_ref[0]
    r = lax.broadcasted_iota(jnp.int32, (c, c), 0)
    cc = lax.broadcasted_iota(jnp.int32, (c, c), 1)
    tri = cc <= r
    strict = cc < r
    eye = (cc == r).astype(F32)
    pair_masks = []
    shift = 0
    while (1 << shift) < c:
        rb = r >> shift
        pair_masks.append(((rb & 1) == 1) & ((cc >> shift) == rb - 1))
        shift += 1
    heads = range(N_HEADS)
    stack = lambda f: jnp.stack([f(h) for h in heads], axis=0)
    hs = lambda h: slice(h * HEAD_DIM, (h + 1) * HEAD_DIM)
    q = stack(lambda h: q_ref[0, :, hs(h)])
    k = stack(lambda h: k_ref[0, :, hs(h)])
    v = stack(lambda h: v_ref[0, :, hs(h)])
    gc = stack(lambda h: cum[:, N_HEADS + h:N_HEADS + h + 1])
    gr = stack(lambda h: cum_t[N_HEADS + h:N_HEADS + h + 1, :])
    bt = stack(lambda h: small[:, 2 * N_HEADS + h:2 * N_HEADS + h + 1])
    g_last = gc[:, c - 1:c, :]
    decay = jnp.where(tri, jnp.exp(jnp.where(tri, gc - gr, 0.0)), 0.0)
    kb = k * bt
    kk_qk = _bmm_nt(jnp.concatenate([kb, q], axis=1).astype(BF16), k.astype(BF16))
    lmat = jnp.where(strict, kk_qk[:, :c] * decay, 0.0)
    a_in = jnp.where(tri, kk_qk[:, c:] * decay, 0.0)
    tm = jnp.broadcast_to(eye, (N_HEADS, c, c))
    for e_mask in pair_masks:
        tb = tm.astype(BF16)
        te = _bmm(tb, jnp.where(e_mask, lmat, 0.0).astype(BF16))
        tm = tm - _bmm(te.astype(BF16), tb)
    eg = jnp.exp(gc)
    uwb = _bmm(tm.astype(BF16), jnp.concatenate([v * bt, kb * eg], axis=2).astype(BF16)).astype(BF16)
    au = _bmm(a_in.astype(BF16), uwb)
    kt = k * jnp.exp(g_last - gc)
    kt_t = stack(lambda h: kt[h].T)
    ku = _bmm(kt_t.astype(BF16), uwb)
    qt = q * eg - au[:, :, HEAD_DIM:]
    s_all = s_ref[...]
    x = _bmm(jnp.concatenate([-ku[:, :, HEAD_DIM:], qt], axis=1).astype(BF16), s_all.astype(BF16))
    s_new = s_all * jnp.exp(g_last) + x[:, :HEAD_DIM] + ku[:, :, :HEAD_DIM]
    s_ref[...] = s_new
    so_ref[0] = s_new
    o = x[:, HEAD_DIM:] + au[:, :, :HEAD_DIM]
    o = o * lax.rsqrt(jnp.mean(o * o, axis=-1, keepdims=True) + RMS_EPS) * nw_ref[...]
    for h in heads:
        zh = z_ref[0, :, hs(h)]
        o_ref[0, :, hs(h)] = (o[h] * (zh * _sigmoid(zh))).astype(BF16)


def gdn_chunks(q, k, v, z, small, cum, s0, norm_w):
    nseq, tp, _ = q.shape
    c = GDN_CHUNK
    blk = lambda w: pl.BlockSpec((1, c, w), lambda s, i: (s, i, 0))
    st = pl.BlockSpec((1, N_HEADS, HEAD_DIM, HEAD_DIM), lambda s, i: (s, 0, 0, 0))
    return pl.pallas_call(
        _gdn_kernel,
        grid=(nseq, tp // c),
        in_specs=[blk(D_HEADS)] * 4 + [blk(LANES)] * 2 + [st, pl.BlockSpec((1, HEAD_DIM), lambda s, i: (0, 0))],
        out_specs=[blk(D_HEADS), st],
        out_shape=[jax.ShapeDtypeStruct((nseq, tp, D_HEADS), BF16),
                   jax.ShapeDtypeStruct((nseq, N_HEADS, HEAD_DIM, HEAD_DIM), F32)],
        scratch_shapes=[pltpu.VMEM((N_HEADS, HEAD_DIM, HEAD_DIM), F32)],
        compiler_params=_cparams(2, 32),
        name="gdn_chunks",
    )(q, k, v, z, small, cum, s0, norm_w)


def _split3(a):
    hi = a.astype(BF16)
    lo = (a - hi.astype(F32)).astype(BF16)
    return hi, lo


def _out_proj_kernel(x_ref, of_ref, og_ref, g1_ref, sc_ref, sh_ref, rw_ref, wt_ref, wb_ref, wr_ref, br_ref,
                     x1_ref, h2_ref, lg_ref):
    g, tm, d = x_ref.shape
    m = g * tm
    mix = _dot(of_ref[...].reshape(m, D_HEADS), wt_ref[...]) + _dot(og_ref[...].reshape(m, D_HEADS), wb_ref[...])
    x1 = x_ref[...] + g1_ref[...] * mix.reshape(g, tm, d)
    x1_ref[...] = x1
    ms = jnp.mean(x1 * x1, axis=-1, keepdims=True)
    h2 = x1 * lax.rsqrt(ms + RMS_EPS) * rw_ref[...]
    h2 = (h2 * (1.0 + sc_ref[...]) + sh_ref[...]).reshape(m, d)
    h2_ref[...] = h2.reshape(g, tm, d)
    hh, hl = _split3(h2)
    wh, wl = _split3(wr_ref[...])
    lg = _dot(hh, wh) + (_dot(hh, wl) + _dot(hl, wh)) + br_ref[...]
    lg_ref[...] = lg.reshape(g, tm, LANES)


def out_proj(x3, o_fox, o_gdn, g1, sc2, sh2, rms_w, w_top, w_bot, w_router_p, b_router_p, *, group, tm):
    nseq, t, d = x3.shape
    g = group
    blk = lambda w: pl.BlockSpec((g, tm, w), lambda s, i: (s, i, 0))
    const2 = lambda a: pl.BlockSpec(a.shape, lambda s, i: (0, 0))
    res = lambda a: pl.BlockSpec(a.shape, lambda s, i: (0, 0), pipeline_mode=pl.Buffered(1))
    mod = pl.BlockSpec((g, 1, d), lambda s, i: (s, 0, 0))
    return pl.pallas_call(
        _out_proj_kernel,
        grid=(nseq // g, t // tm),
        in_specs=[blk(d), blk(D_HEADS), blk(D_HEADS), mod, mod, mod, const2(rms_w),
                  res(w_top), res(w_bot), const2(w_router_p), const2(b_router_p)],
        out_specs=[blk(d), blk(d), blk(LANES)],
        out_shape=[jax.ShapeDtypeStruct((nseq, t, d), F32), jax.ShapeDtypeStruct((nseq, t, d), F32),
                   jax.ShapeDtypeStruct((nseq, t, LANES), F32)],
        compiler_params=_cparams(2, 40),
        name="out_proj",
    )(x3, o_fox, o_gdn, g1, sc2, sh2, rms_w, w_top, w_bot, w_router_p, b_router_p)


def _moe_dispatch_kernel(tok_ref, nu_ref, src_hbm, o_ref, buf_ref, sem):
    i = pl.program_id(0)
    nu = nu_ref[0]
    rows = buf_ref.shape[1]

    def issue_block(blk):
        slot = blk & 1
        base = blk * rows

        def issue(g, carry):
            for u in range(SUBLANES):
                r = g * SUBLANES + u
                pltpu.make_async_copy(src_hbm.at[pl.ds(tok_ref[base + r], 1)], buf_ref.at[slot, pl.ds(r, 1)],
                                      sem.at[slot]).start(priority=u % 2)
            return carry

        lax.fori_loop(0, rows // SUBLANES, issue, 0)

    @pl.when((i == 0) & (nu > 0))
    def _():
        issue_block(0)

    @pl.when(i + 1 < nu)
    def _():
        issue_block(i + 1)

    @pl.when(i < nu)
    def _():
        slot = i & 1
        pltpu.make_async_copy(src_hbm.at[pl.ds(0, rows)], buf_ref.at[slot], sem.at[slot]).wait()
        o_ref[...] = buf_ref[slot].astype(BF16)

    @pl.when(i >= nu)
    def _():
        o_ref[...] = jnp.zeros(o_ref.shape, o_ref.dtype)


def moe_dispatch(h2_all, row_tok, n_used):
    n, d = h2_all.shape
    r = row_tok.shape[0]
    gs = pltpu.PrefetchScalarGridSpec(
        num_scalar_prefetch=2,
        grid=(r // MOE_ROWS,),
        in_specs=[pl.BlockSpec(memory_space=pl.ANY)],
        out_specs=pl.BlockSpec((MOE_ROWS, d), lambda i, tok, nu: (i, 0)),
        scratch_shapes=[pltpu.VMEM((2, MOE_ROWS, d), F32), pltpu.SemaphoreType.DMA((2,))])
    return pl.pallas_call(
        _moe_dispatch_kernel, grid_spec=gs,
        out_shape=jax.ShapeDtypeStruct((r, d), BF16),
        compiler_params=_cparams(1, 16),
        name="moe_dispatch",
    )(row_tok, n_used, h2_all)


def _run_weights(i, col_offsets, meta, w_hbm, wf_ref, wb_refs, sem):
    be_ref, first_ref, par_ref, nxt_ref = meta
    tn = wf_ref.shape[3]

    def copies(e, slot):
        return [pltpu.make_async_copy(w_hbm.at[e, :, pl.ds(pl.multiple_of(c0, tn), tn)], wf_ref.at[slot, k],
                                      sem.at[slot]) for k, c0 in enumerate(col_offsets)]

    @pl.when(first_ref[i] == 1)
    def _():
        slot = par_ref[i]

        @pl.when(i == 0)
        def _():
            for cp in copies(be_ref[i], slot):
                cp.start()

        for cp in copies(be_ref[i], slot):
            cp.wait()
        for k, wb_ref in enumerate(wb_refs):
            wb_ref[...] = wf_ref[slot, k].astype(BF16)

        @pl.when(nxt_ref[i] >= 0)
        def _():
            for cp in copies(nxt_ref[i], 1 - slot):
                cp.start()


def _moe_gu_kernel(be_ref, first_ref, par_ref, nxt_ref, nu_ref, x_ref, w_hbm, bg_ref, bl_ref, h_ref,
                   wf_ref, wgb_ref, wlb_ref, sem):
    j = pl.program_id(0)
    i = pl.program_id(1)
    tn = h_ref.shape[1]
    nj = pl.num_programs(0)

    @pl.when(i < nu_ref[0])
    def _():
        _run_weights(i, (j * tn, (nj + j) * tn), (be_ref, first_ref, par_ref, nxt_ref), w_hbm, wf_ref,
                     (wgb_ref, wlb_ref), sem)
        x = x_ref[...]
        glu = jnp.minimum(_dot(x, wgb_ref[...]) + bg_ref[0], SWIGLU_LIMIT)
        lin = jnp.clip(_dot(x, wlb_ref[...]) + bl_ref[0], -SWIGLU_LIMIT, SWIGLU_LIMIT)
        h_ref[...] = (glu * _sigmoid(SWIGLU_ALPHA * glu) * (lin + 1.0)).astype(BF16)

    @pl.when(i >= nu_ref[0])
    def _():
        h_ref[...] = jnp.zeros(h_ref.shape, h_ref.dtype)


def _moe_specs(rows, d_in, tn, n_bias, nj):
    blk_i = lambda i, nu: jnp.minimum(i, nu[0] - 1)
    x_spec = pl.BlockSpec((rows, d_in), lambda j, i, be, fi, pa, nx, nu: (blk_i(i, nu), 0))
    bias = [pl.BlockSpec((1, 1, tn), (lambda j, i, be, fi, pa, nx, nu, k=k: (be[blk_i(i, nu)], 0, k * nj + j)))
            for k in range(n_bias)]
    out_spec = pl.BlockSpec((rows, tn), lambda j, i, be, fi, pa, nx, nu: (i, j))
    return x_spec, bias, out_spec


def moe_gate_up(x_sorted, meta, n_used, w_gu, b_gu3, *, tn):
    r, d = x_sorted.shape
    d_ff = w_gu.shape[2] // 2
    nj = d_ff // tn
    x_spec, bias, out_spec = _moe_specs(MOE_ROWS, d, tn, 2, nj)
    gs = pltpu.PrefetchScalarGridSpec(
        num_scalar_prefetch=5,
        grid=(nj, r // MOE_ROWS),
        in_specs=[x_spec, pl.BlockSpec(memory_space=pl.ANY)] + bias,
        out_specs=out_spec,
        scratch_shapes=[pltpu.VMEM((2, 2, d, tn), F32), pltpu.VMEM((d, tn), BF16), pltpu.VMEM((d, tn), BF16),
                        pltpu.SemaphoreType.DMA((2,))])
    return pl.pallas_call(
        _moe_gu_kernel, grid_spec=gs,
        out_shape=jax.ShapeDtypeStruct((r, d_ff), BF16),
        compiler_params=_cparams(2, 52),
        name="moe_gate_up",
    )(*meta, n_used, x_sorted, w_gu, b_gu3, b_gu3)


def _moe_down_kernel(be_ref, first_ref, par_ref, nxt_ref, nu_ref, h_ref, w_hbm, b_ref, o_ref, wf_ref, wb_ref, sem):
    j = pl.program_id(0)
    i = pl.program_id(1)
    tn = o_ref.shape[1]

    @pl.when(i < nu_ref[0])
    def _():
        _run_weights(i, (j * tn,), (be_ref, first_ref, par_ref, nxt_ref), w_hbm, wf_ref, (wb_ref,), sem)
        o_ref[...] = _dot(h_ref[...], wb_ref[...]) + b_ref[0]

    @pl.when(i >= nu_ref[0])
    def _():
        o_ref[...] = jnp.zeros(o_ref.shape, o_ref.dtype)


def moe_down(h_sorted, meta, n_used, w_down, b_down3, *, tn):
    r, d_ff = h_sorted.shape
    d = w_down.shape[2]
    x_spec, bias, out_spec = _moe_specs(MOE_ROWS, d_ff, tn, 1, d // tn)
    gs = pltpu.PrefetchScalarGridSpec(
        num_scalar_prefetch=5,
        grid=(d // tn, r // MOE_ROWS),
        in_specs=[x_spec, pl.BlockSpec(memory_space=pl.ANY)] + bias,
        out_specs=out_spec,
        scratch_shapes=[pltpu.VMEM((2, 1, d_ff, tn), F32), pltpu.VMEM((d_ff, tn), BF16),
                        pltpu.SemaphoreType.DMA((2,))])
    return pl.pallas_call(
        _moe_down_kernel, grid_spec=gs,
        out_shape=jax.ShapeDtypeStruct((r, d), F32),
        compiler_params=_cparams(2, 40),
        name="moe_down",
    )(*meta, n_used, h_sorted, w_down, b_down3)


def _route(logits):
    n = logits.shape[0]
    a = n * TOP_K
    top_val, top_idx = lax.top_k(logits, TOP_K)
    gate = jax.nn.softmax(top_val, axis=-1)
    flat_e = top_idx.reshape(a).astype(jnp.int32)
    experts = jnp.arange(N_EXPERTS, dtype=jnp.int32)
    order = jnp.argsort(flat_e).astype(jnp.int32)
    counts = jnp.sum((flat_e[:, None] == experts[None, :]).astype(jnp.int32), axis=0)
    starts = jnp.cumsum(counts) - counts
    padded = -(-counts // MOE_ROWS) * MOE_ROWS
    pad_ends = jnp.cumsum(padded)
    pad_starts = pad_ends - padded
    sorted_e = flat_e[order]
    dest = pad_starts[sorted_e] + jnp.arange(a, dtype=jnp.int32) - starts[sorted_e]
    pos = dest[jnp.argsort(order)].reshape(n, TOP_K)
    nb = -(-(a + N_EXPERTS * (MOE_ROWS - 1)) // MOE_ROWS)
    rows = jnp.arange(nb * MOE_ROWS, dtype=jnp.int32)
    row_e = jnp.minimum(jnp.sum((pad_ends[None, :] <= rows[:, None]).astype(jnp.int32), axis=1), N_EXPERTS - 1)
    off = rows - pad_starts[row_e]
    src = jnp.clip(starts[row_e] + off, 0, a - 1)
    row_tok = jnp.where(off < counts[row_e], order[src] // TOP_K, 0).astype(jnp.int32)
    n_used = pad_ends[-1] // MOE_ROWS
    blocks = jnp.arange(nb, dtype=jnp.int32)
    block_e = row_e[::MOE_ROWS]
    used = blocks < n_used
    first = used & ((blocks == 0) | (block_e != jnp.roll(block_e, 1)))
    parity = (jnp.cumsum(first.astype(jnp.int32)) - 1) & 1
    later_first = first[None, :] & (blocks[None, :] > blocks[:, None])
    next_first = jnp.min(jnp.where(later_first, blocks[None, :], nb), axis=1)
    next_e = jnp.where(next_first < nb, block_e[jnp.minimum(next_first, nb - 1)], -1)
    meta = (block_e.astype(jnp.int32), first.astype(jnp.int32), parity.astype(jnp.int32), next_e.astype(jnp.int32))
    return row_tok, gate, meta, n_used.astype(jnp.int32).reshape(1), pos


def _forward(x_prompt, x_sample, cache_k, cache_v, cache_logf, state_gdn, state_conv, page_table,
             c_prompt, c_sample, w_ada, b_ada, rms_mix, rms_ffn, w_in, b_forget, q_norm, k_norm,
             conv_w, a_log, dt_bias, gdn_norm, w_out, w_router, b_router, w_gu, b_gu, w_down, b_down):
    depth = w_ada.shape[0]
    assert depth == 1, "single-layer trunk"
    l = 0
    bp, s, d = x_prompt.shape
    bd, t, _ = x_sample.shape
    n_pool = cache_k.shape[1]
    tm_p = 256

    n_c = bp + bd
    c_all = jnp.concatenate([c_prompt, c_sample, jnp.zeros((-n_c % SUBLANES, d), F32)], axis=0)
    mod = ada_mod(c_all, w_ada[l], b_ada[l])
    modp = mod[:bp].reshape(bp, 1, N_ADA, d)
    mods = mod[bp:n_c].reshape(bd, 1, N_ADA, d)

    w = w_in[l]
    o_ff, o_g, o_ab = 3 * D_HEADS, 3 * D_HEADS + N_HEADS, 7 * D_HEADS + N_HEADS
    w_small = jnp.concatenate([w[:, o_ff:o_g], w[:, o_ab:o_ab + 2 * N_HEADS]], axis=1)
    w_small = jnp.pad(w_small, ((0, 0), (0, LANES - 3 * N_HEADS)))
    w_fox = jnp.concatenate([w[:, :o_ff], w_small], axis=1).astype(BF16)
    w_g = w[:, o_g:o_ab].astype(BF16)
    qn_t = jnp.tile(q_norm[l], N_HEADS).reshape(1, D_HEADS)
    kn_t = jnp.tile(k_norm[l], N_HEADS).reshape(1, D_HEADS)
    pvec = jnp.zeros((SUBLANES, LANES), F32)
    pvec = pvec.at[0, 0:N_HEADS].set(b_forget[l]).at[0, N_HEADS:2 * N_HEADS].set(dt_bias[l])
    pvec = pvec.at[1, N_HEADS:2 * N_HEADS].set(a_log[l])
    rw1 = rms_mix[l].reshape(1, d)
    rw2 = rms_ffn[l].reshape(1, d)
    w_top = w_out[l][:D_HEADS].astype(BF16)
    w_bot = w_out[l][D_HEADS:].astype(BF16)
    w_router_p = jnp.pad(w_router[l], ((0, 0), (0, LANES - N_EXPERTS)))
    b_router_p = jnp.pad(b_router[l], (0, LANES - N_EXPERTS)).reshape(1, LANES)
    nw = gdn_norm[l].reshape(1, HEAD_DIM)

    def mixer_inputs(x3, m4, conv_in, group, tm):
        sh1, sc1 = m4[:, :, 0], m4[:, :, 1]
        pf = proj_fox(x3, sc1, sh1, rw1, w_fox, qn_t, kn_t, pvec, group=group, tm=tm)
        pg = proj_gdn(x3, sc1, sh1, rw1, w_g, conv_w[l], conv_in, group=group, tm=tm)
        return pf, pg

    pf, pg = mixer_inputs(x_prompt, modp, jnp.zeros((bp, SUBLANES, 3 * D_HEADS), F32), 1, tm_p)
    qb_p, k_p, v_p, kb_p, vb_p, small_p, cum_p = pf
    gq_p, gk_p, gv_p, gz_p, cst_p = pg
    f_t = cum_p[..., :N_HEADS].transpose(0, 2, 1).reshape(bp * N_HEADS, s)
    o_fox_p = fox_prompt_attention(qb_p, kb_p, vb_p, f_t.reshape(bp * N_HEADS, s, 1),
                                   f_t.reshape(bp * N_HEADS, 1, s), tq=512)
    o_gdn_p, s_p = gdn_chunks(gq_p, gk_p, gv_p, gz_p, small_p, cum_p,
                              jnp.zeros((bp, N_HEADS, HEAD_DIM, HEAD_DIM), F32), nw)
    x1_p, h2_p, lg_p = out_proj(x_prompt, o_fox_p, o_gdn_p, modp[:, :, 2], modp[:, :, 4], modp[:, :, 3],
                                rw2, w_top, w_bot, w_router_p, b_router_p, group=1, tm=tm_p)

    conv_in = jnp.pad(state_conv[l], ((0, 0), (SUBLANES - (CONV_W - 1), 0), (0, 0)))
    sf, sg = mixer_inputs(x_sample, mods, conv_in, bd, t)
    qb_s, k_s, v_s, kb_s, vb_s, small_s, cum_s = sf
    gq_s, gk_s, gv_s, gz_s, cst_s = sg
    hq = N_HEADS * t
    cs = cum_s[..., :N_HEADS]
    csn = jnp.pad(cs.reshape(bd, 1, hq), ((0, 0), (0, 0), (0, LANES - hq)))
    csq = cs.transpose(0, 2, 1).reshape(bd, hq, 1)
    q_rows = qb_s.reshape(bd, t, N_HEADS, HEAD_DIM).transpose(0, 2, 1, 3).reshape(bd, hq, HEAD_DIM)
    new_rows = lambda a: jnp.pad(a.reshape(bd, hq, HEAD_DIM), ((0, 0), (0, LANES - hq), (0, 0)))
    lf_rows = cache_logf[l].reshape(n_pool, 1, PAGE_SIZE * N_HEADS)
    o_rows = fox_sample_attention(q_rows, cache_k, cache_v, lf_rows, page_table, new_rows(kb_s), new_rows(vb_s),
                                  csn, csq, pp=8, t=t)
    o_fox_s = o_rows.reshape(bd, N_HEADS, t, HEAD_DIM).transpose(0, 2, 1, 3).reshape(bd, t, D_HEADS)
    padc = lambda a: jnp.pad(a, ((0, 0), (0, GDN_CHUNK - t), (0, 0)))
    cum_s_pad = jnp.pad(cum_s, ((0, 0), (0, GDN_CHUNK - t), (0, 0)), mode="edge")
    o_gdn_s, s_s = gdn_chunks(padc(gq_s), padc(gk_s), padc(gv_s), padc(gz_s), padc(small_s), cum_s_pad,
                              state_gdn[l].astype(F32), nw)
    x1_s, h2_s, lg_s = out_proj(x_sample, o_fox_s, o_gdn_s[:, :t], mods[:, :, 2], mods[:, :, 4], mods[:, :, 3],
                                rw2, w_top, w_bot, w_router_p, b_router_p, group=bd, tm=t)

    n_p, n_s = bp * s, bd * t
    h2_all = jnp.concatenate([h2_p.reshape(n_p, d), h2_s.reshape(n_s, d)], axis=0)
    logits = jnp.concatenate([lg_p.reshape(n_p, LANES), lg_s.reshape(n_s, LANES)], axis=0)[:, :N_EXPERTS]
    row_tok, gate, meta, n_used, pos = _route(logits)
    x_sorted = moe_dispatch(h2_all, row_tok, n_used)
    hid = moe_gate_up(x_sorted, meta, n_used, w_gu[l], b_gu[l].reshape(N_EXPERTS, 1, -1), tn=512)
    outg = moe_down(hid, meta, n_used, w_down[l], b_down[l].reshape(N_EXPERTS, 1, -1), tn=1024)
    moe = sum(gate[:, k:k + 1] * outg[pos[:, k]] for k in range(TOP_K))
    y_p = x1_p + modp[:, :, 5] * moe[:n_p].reshape(bp, s, d)
    y_s = x1_s + mods[:, :, 5] * moe[n_p:].reshape(bd, t, d)

    hd = lambda a, b_, t_: a.reshape(1, b_, t_, N_HEADS, HEAD_DIM)
    return (y_p, y_s,
            hd(k_p, bp, s), hd(v_p, bp, s), small_p[..., :N_HEADS][None],
            s_p[None], cst_p[:, SUBLANES - (CONV_W - 1):][None],
            hd(k_s, bd, t), hd(v_s, bd, t), small_s[..., :N_HEADS][None],
            s_s[None], cst_s[:, SUBLANES - (CONV_W - 1):][None])


def kernel(x_prompt, x_sample, cache_k, cache_v, cache_logf, state_gdn, state_conv, page_table, c_prompt, c_sample, w_ada, b_ada, rms_mix, rms_ffn, w_in, b_forget, q_norm, k_norm, conv_w, a_log, dt_bias, gdn_norm, w_out, w_router, b_router, w_gu, b_gu, w_down, b_down):
    return _forward(x_prompt, x_sample, cache_k, cache_v, cache_logf, state_gdn, state_conv, page_table,
                    c_prompt, c_sample, w_ada, b_ada, rms_mix, rms_ffn, w_in, b_forget, q_norm, k_norm,
                    conv_w, a_log, dt_bias, gdn_norm, w_out, w_router, b_router, w_gu, b_gu, w_down, b_down)
```

```python
import functools

import jax
import jax.numpy as jnp
from jax import lax
from jax.experimental import pallas as pl
from jax.experimental.pallas import tpu as pltpu

F32 = jnp.float32
BF16 = jnp.bfloat16

D_MODEL = 2048
HEAD_DIM = 128
N_HEADS = 8
D_HEADS = N_HEADS * HEAD_DIM
CONV_W = 4
PAGE_SIZE = 128
N_EXPERTS = 32
TOP_K = 4
N_ADA = 6
RMS_EPS = 1e-6
L2_EPS = 1e-6
SWIGLU_ALPHA = 1.702
SWIGLU_LIMIT = 7.0

LANES = 128
SUBLANES = 8
GDN_CHUNK = 128
MOE_ROWS = 256
NEG_BIG = -1e30

PROJ_ROWS = 256
ATTN_TQ = 512
SAMPLE_PAGES = 8
MOE_GU_TN = 512
MOE_DOWN_TN = 1024
DISPATCH_UNROLL = 16

_ARB = "arbitrary"


def _cparams(n_axes, vmem_mb):
    return pltpu.CompilerParams(dimension_semantics=(_ARB,) * n_axes,
                                vmem_limit_bytes=vmem_mb * 1024 * 1024)


def _dot(a, b):
    return jnp.dot(a, b, preferred_element_type=F32)


def _dot_nt(a, b):
    return lax.dot_general(a, b, (((1,), (1,)), ((), ())), preferred_element_type=F32)


def _bmm(a, b):
    return lax.dot_general(a, b, (((2,), (1,)), ((0,), (0,))), preferred_element_type=F32)


def _bmm_nt(a, b):
    return lax.dot_general(a, b, (((2,), (2,)), ((0,), (0,))), preferred_element_type=F32)


def _softplus(z):
    return jnp.maximum(z, 0.0) + jnp.log(1.0 + jnp.exp(-jnp.abs(z)))


def _sigmoid(z):
    return 1.0 / (1.0 + jnp.exp(-z))


def _ada_kernel(c_ref, w_ref, b_ref, o_ref):
    c = c_ref[...]
    a = (c * _sigmoid(c)).astype(BF16)
    o_ref[...] = _dot(a, w_ref[...].astype(BF16)) + b_ref[...]


def ada_mod(c_all, w_ada, b_ada):
    m, d = c_all.shape
    n = w_ada.shape[1]
    tn = min(1024, n)
    return pl.pallas_call(
        _ada_kernel,
        grid=(n // tn,),
        in_specs=[pl.BlockSpec((m, d), lambda j: (0, 0)),
                  pl.BlockSpec((d, tn), lambda j: (0, j)),
                  pl.BlockSpec((1, tn), lambda j: (0, j))],
        out_specs=pl.BlockSpec((m, tn), lambda j: (0, j)),
        out_shape=jax.ShapeDtypeStruct((m, n), F32),
        compiler_params=_cparams(1, 40),
        name="ada_mod",
    )(c_all, w_ada, b_ada.reshape(1, n))


def _mod_norm_bf16(x_ref, sc_ref, sh_ref, rw_ref):
    x = x_ref[...]
    g, tm, d = x.shape
    ms = jnp.mean(x * x, axis=-1, keepdims=True)
    h = x * lax.rsqrt(ms + RMS_EPS) * rw_ref[...]
    h = h * (1.0 + sc_ref[...]) + sh_ref[...]
    return h.reshape(g * tm, d).astype(BF16)


def _per_head(p, fn):
    return jnp.concatenate([fn(p[:, h * HEAD_DIM:(h + 1) * HEAD_DIM]) for h in range(N_HEADS)], axis=-1)


def _rms_head(s):
    return s * lax.rsqrt(jnp.mean(s * s, axis=-1, keepdims=True) + RMS_EPS)


def _l2_head(s):
    return s * lax.rsqrt(jnp.sum(s * s, axis=-1, keepdims=True) + L2_EPS)


def _proj_fox_kernel(x_ref, sc_ref, sh_ref, rw_ref, w_ref, qn_ref, kn_ref, pv_ref,
                     q_ref, k_ref, v_ref, kb_ref, vb_ref, small_ref, cum_ref, carry_ref, *, seg):
    i = pl.program_id(1)
    g, tm, _ = x_ref.shape
    m = g * tm
    hb = _mod_norm_bf16(x_ref, sc_ref, sh_ref, rw_ref)

    qn = _per_head(_dot(hb, w_ref[:, 0:D_HEADS]), _rms_head) * qn_ref[...]
    q_ref[...] = (qn * HEAD_DIM ** -0.5).astype(BF16).reshape(g, tm, D_HEADS)
    kn = _per_head(_dot(hb, w_ref[:, D_HEADS:2 * D_HEADS]), _rms_head) * kn_ref[...]
    k_ref[...] = kn.reshape(g, tm, D_HEADS)
    kb_ref[...] = kn.astype(BF16).reshape(g, tm, D_HEADS)
    vv = _dot(hb, w_ref[:, 2 * D_HEADS:3 * D_HEADS])
    v_ref[...] = vv.reshape(g, tm, D_HEADS)
    vb_ref[...] = vv.astype(BF16).reshape(g, tm, D_HEADS)

    z = _dot(hb, w_ref[:, 3 * D_HEADS:3 * D_HEADS + LANES]) + pv_ref[0:1, :]
    lane = lax.broadcasted_iota(jnp.int32, (m, LANES), 1)
    sp = _softplus(z)
    logsig = z - sp
    gval = -jnp.exp(pv_ref[1:2, :]) * sp
    is_f = lane < N_HEADS
    is_g = (lane >= N_HEADS) & (lane < 2 * N_HEADS)
    is_b = (lane >= 2 * N_HEADS) & (lane < 3 * N_HEADS)
    small = jnp.where(is_f, logsig, jnp.where(is_g, gval, jnp.where(is_b, _sigmoid(z), 0.0)))
    small_ref[...] = small.reshape(g, tm, LANES)

    row = lax.broadcasted_iota(jnp.int32, (m, LANES), 0)
    pos = row & (tm - 1)
    rowmod = jnp.where(is_g, pos & (seg - 1), pos)
    c = small
    s = 1
    while s < tm:
        c = c + jnp.where(rowmod >= s, pltpu.roll(c, s, axis=0), 0.0)
        s *= 2
    lane3 = lax.broadcasted_iota(jnp.int32, (g, tm, LANES), 2)
    carry = jnp.where(i == 0, 0.0, carry_ref[...])
    c3 = c.reshape(g, tm, LANES) + jnp.where(lane3 < N_HEADS, carry, 0.0)
    cum_ref[...] = c3
    carry_ref[...] = c3[:, tm - 1:tm, :]


def proj_fox(x3, sc, sh, rms_w, w_fox, qn_t, kn_t, pvec, *, group, tm):
    nseq, t, d = x3.shape
    g = group
    seg = min(GDN_CHUNK, tm)
    grid = (nseq // g, t // tm)
    blk = lambda w: pl.BlockSpec((g, tm, w), lambda s, i: (s, i, 0))
    const2 = lambda a: pl.BlockSpec(a.shape, lambda s, i: (0, 0))
    mod = pl.BlockSpec((g, 1, d), lambda s, i: (s, 0, 0))
    outs = [jax.ShapeDtypeStruct((nseq, t, D_HEADS), BF16),
            jax.ShapeDtypeStruct((nseq, t, D_HEADS), F32),
            jax.ShapeDtypeStruct((nseq, t, D_HEADS), F32),
            jax.ShapeDtypeStruct((nseq, t, D_HEADS), BF16),
            jax.ShapeDtypeStruct((nseq, t, D_HEADS), BF16),
            jax.ShapeDtypeStruct((nseq, t, LANES), F32),
            jax.ShapeDtypeStruct((nseq, t, LANES), F32)]
    return pl.pallas_call(
        functools.partial(_proj_fox_kernel, seg=seg),
        grid=grid,
        in_specs=[blk(d), mod, mod, const2(rms_w),
                  pl.BlockSpec(w_fox.shape, lambda s, i: (0, 0), pipeline_mode=pl.Buffered(1)),
                  const2(qn_t), const2(kn_t), const2(pvec)],
        out_specs=[blk(D_HEADS)] * 5 + [blk(LANES)] * 2,
        out_shape=outs,
        scratch_shapes=[pltpu.VMEM((g, 1, LANES), F32)],
        compiler_params=_cparams(2, 48),
        name="proj_fox",
    )(x3, sc, sh, rms_w, w_fox, qn_t, kn_t, pvec)


def _proj_gdn_kernel(x_ref, sc_ref, sh_ref, rw_ref, w_ref, cw_ref, cin_ref,
                     q_ref, k_ref, v_ref, z_ref, cst_ref, cbuf_ref):
    i = pl.program_id(1)
    g, tm, _ = x_ref.shape
    hb = _mod_norm_bf16(x_ref, sc_ref, sh_ref, rw_ref)
    outs = (q_ref, k_ref, v_ref)
    for c in range(3):
        cols = slice(c * D_HEADS, (c + 1) * D_HEADS)
        pc = _dot(hb, w_ref[:, cols]).reshape(g, tm, D_HEADS)

        @pl.when(i == 0)
        def _():
            cbuf_ref[c, :, 0:SUBLANES, :] = cin_ref[:, :, cols]

        @pl.when(i > 0)
        def _():
            cbuf_ref[c, :, 0:SUBLANES, :] = cbuf_ref[c, :, tm:tm + SUBLANES, :]

        cbuf_ref[c, :, SUBLANES:tm + SUBLANES, :] = pc
        cst_ref[:, :, cols] = pc[:, tm - SUBLANES:tm, :]
        base = SUBLANES - (CONV_W - 1)
        y = None
        for j in range(CONV_W):
            term = cbuf_ref[c, :, base + j:base + j + tm, :] * cw_ref[j:j + 1, cols]
            y = term if y is None else y + term
        y = (y * _sigmoid(y)).reshape(g * tm, D_HEADS)
        if c == 0:
            y = _per_head(y, _l2_head) * HEAD_DIM ** -0.5
        elif c == 1:
            y = _per_head(y, _l2_head)
        outs[c][...] = y.reshape(g, tm, D_HEADS)
    z_ref[...] = _dot(hb, w_ref[:, 3 * D_HEADS:4 * D_HEADS]).reshape(g, tm, D_HEADS)


def proj_gdn(x3, sc, sh, rms_w, w_g, conv_w, conv_in, *, group, tm):
    nseq, t, d = x3.shape
    g = group
    grid = (nseq // g, t // tm)
    blk = lambda w: pl.BlockSpec((g, tm, w), lambda s, i: (s, i, 0))
    const2 = lambda a: pl.BlockSpec(a.shape, lambda s, i: (0, 0))
    mod = pl.BlockSpec((g, 1, d), lambda s, i: (s, 0, 0))
    st = pl.BlockSpec((g, SUBLANES, 3 * D_HEADS), lambda s, i: (s, 0, 0))
    outs = [jax.ShapeDtypeStruct((nseq, t, D_HEADS), F32)] * 4 + \
           [jax.ShapeDtypeStruct((nseq, SUBLANES, 3 * D_HEADS), F32)]
    return pl.pallas_call(
        _proj_gdn_kernel,
        grid=grid,
        in_specs=[blk(d), mod, mod, const2(rms_w),
                  pl.BlockSpec(w_g.shape, lambda s, i: (0, 0), pipeline_mode=pl.Buffered(1)),
                  const2(conv_w), st],
        out_specs=[blk(D_HEADS)] * 4 + [st],
        out_shape=outs,
        scratch_shapes=[pltpu.VMEM((3, g, tm + SUBLANES, D_HEADS), F32)],
        compiler_params=_cparams(2, 52),
        name="proj_gdn",
    )(x3, sc, sh, rms_w, w_g, conv_w, conv_in)


def _fox_prompt_kernel(q_ref, k_ref, v_ref, fc_ref, fr_ref, o_ref, *, tq):
    qi = pl.program_id(2)
    q = q_ref[0]
    fq = fc_ref[0]

    def step(j, carry, masked):
        m_i, l_i, acc = carry
        off = pl.multiple_of(j * tq, tq)
        ks = k_ref[0, pl.ds(off, tq), :]
        vs = v_ref[0, pl.ds(off, tq), :]
        s = _dot_nt(q, ks) + (fq - fr_ref[0, :, pl.ds(off, tq)])
        if masked:
            r = lax.broadcasted_iota(jnp.int32, (tq, tq), 0)
            c = lax.broadcasted_iota(jnp.int32, (tq, tq), 1)
            s = jnp.where(c <= r, s, NEG_BIG)
        m_new = jnp.maximum(m_i, jnp.max(s, axis=-1, keepdims=True))
        alpha = jnp.exp(m_i - m_new)
        p = jnp.exp(s - m_new)
        l_new = alpha * l_i + jnp.sum(p, axis=-1, keepdims=True)
        acc = alpha * acc + _dot(p.astype(BF16), vs)
        return m_new, l_new, acc

    init = (jnp.full((tq, 1), NEG_BIG, F32), jnp.zeros((tq, 1), F32), jnp.zeros((tq, HEAD_DIM), F32))
    carry = lax.fori_loop(0, qi, functools.partial(step, masked=False), init)
    _, l_i, acc = step(qi, carry, True)
    o_ref[0] = (acc / l_i).astype(BF16)


def fox_prompt_attention(qb, kb, vb, f_col, f_row, *, tq):
    b, s, _ = qb.shape
    grid = (b, N_HEADS, s // tq)
    return pl.pallas_call(
        functools.partial(_fox_prompt_kernel, tq=tq),
        grid=grid,
        in_specs=[pl.BlockSpec((1, tq, HEAD_DIM), lambda bi, h, qi: (bi, qi, h)),
                  pl.BlockSpec((1, s, HEAD_DIM), lambda bi, h, qi: (bi, 0, h)),
                  pl.BlockSpec((1, s, HEAD_DIM), lambda bi, h, qi: (bi, 0, h)),
                  pl.BlockSpec((1, tq, 1), lambda bi, h, qi: (bi * N_HEADS + h, qi, 0)),
                  pl.BlockSpec((1, 1, s), lambda bi, h, qi: (bi * N_HEADS + h, 0, 0))],
        out_specs=pl.BlockSpec((1, tq, HEAD_DIM), lambda bi, h, qi: (bi, qi, h)),
        out_shape=jax.ShapeDtypeStruct((b, s, D_HEADS), BF16),
        compiler_params=_cparams(3, 32),
        name="fox_prompt_attention",
    )(qb, kb, vb, f_col, f_row)


def _fox_sample_kernel(pt_ref, q_ref, *refs, pp, t):
    k_refs = refs[0:pp]
    v_refs = refs[pp:2 * pp]
    lf_refs = refs[2 * pp:3 * pp]
    kn_ref, vn_ref, csn_ref, csq_ref, o_ref, m_ref, l_ref, acc_ref, carry_ref = refs[3 * pp:]
    s = pl.program_id(1)
    ns = pl.num_programs(1)
    hq = N_HEADS * t
    ncol = PAGE_SIZE * N_HEADS

    @pl.when(s == 0)
    def _():
        m_ref[...] = jnp.full(m_ref.shape, NEG_BIG, F32)
        l_ref[...] = jnp.zeros(l_ref.shape, F32)
        acc_ref[...] = jnp.zeros(acc_ref.shape, F32)
        carry_ref[...] = jnp.zeros(carry_ref.shape, F32)

    q = q_ref[0]
    csq = csq_ref[0]

    def update(sc, vals, width):
        m_old = m_ref[...]
        m_new = jnp.maximum(m_old, jnp.max(sc, axis=-1, keepdims=True))
        alpha = jnp.exp(m_old - m_new)
        p = jnp.exp(sc - m_new)
        l_ref[...] = alpha * l_ref[...] + jnp.sum(p, axis=-1, keepdims=True)
        pv = None
        for j, vv in enumerate(vals):
            term = _dot(p[:, j * width:(j + 1) * width].astype(BF16), vv)
            pv = term if pv is None else pv + term
        acc_ref[...] = alpha * acc_ref[...] + pv
        m_ref[...] = m_new

    lane = lax.broadcasted_iota(jnp.int32, (1, ncol), 1)
    carry = carry_ref[...]
    bias = [None] * pp
    for j in reversed(range(pp)):
        lf = lf_refs[j][...]
        c = lf
        step = N_HEADS
        while step < ncol:
            c = c + jnp.where(lane + step < ncol, pltpu.roll(c, ncol - step, axis=1), 0.0)
            step *= 2
        bias[j] = c - lf + carry
        tot = jnp.where(lane < N_HEADS, c, 0.0)
        step = N_HEADS
        while step < ncol:
            tot = tot + pltpu.roll(tot, step, axis=1)
            step *= 2
        carry = carry + tot
    carry_ref[...] = carry

    row = lax.broadcasted_iota(jnp.int32, (hq, ncol), 0)
    col = lax.broadcasted_iota(jnp.int32, (hq, ncol), 1)
    t_shift = t.bit_length() - 1
    h_shift = N_HEADS.bit_length() - 1
    same_head = (col & (N_HEADS - 1)) == (row >> t_shift)
    sc = [jnp.where(same_head, _dot_nt(q, k_refs[j][...].reshape(ncol, HEAD_DIM).astype(BF16)) + (csq + bias[j]),
                    NEG_BIG) for j in range(pp)]
    update(jnp.concatenate(sc, axis=1),
           [v_refs[j][...].reshape(ncol, HEAD_DIM).astype(BF16) for j in range(pp)], ncol)

    @pl.when(s == ns - 1)
    def _():
        rn = lax.broadcasted_iota(jnp.int32, (hq, LANES), 0)
        cn = lax.broadcasted_iota(jnp.int32, (hq, LANES), 1)
        ok = ((cn & (N_HEADS - 1)) == (rn >> t_shift)) & ((cn >> h_shift) <= (rn & (t - 1)))
        sn = jnp.where(ok, _dot_nt(q, kn_ref[0]) + (csq - csn_ref[0]), NEG_BIG)
        update(sn, [vn_ref[0]], LANES)
        o_ref[0] = (acc_ref[...] / l_ref[...]).astype(BF16)


def fox_sample_attention(q_rows, cache_k, cache_v, lf_rows, page_table, kn_pad, vn_pad, csn, csq, *, pp, t):
    bd, hq, _ = q_rows.shape
    n_pages = page_table.shape[1]
    ns = n_pages // pp
    ncol = PAGE_SIZE * N_HEADS
    page_of = lambda b, s, j, pt: pt[b, (ns - 1 - s) * pp + j]
    kv_spec = lambda j: pl.BlockSpec((None, None, PAGE_SIZE, N_HEADS, HEAD_DIM),
                                     lambda b, s, pt: (0, page_of(b, s, j, pt), 0, 0, 0))
    lf_spec = lambda j: pl.BlockSpec((None, 1, ncol), lambda b, s, pt: (page_of(b, s, j, pt), 0, 0))
    per_b = lambda shp: pl.BlockSpec((1,) + shp, lambda b, s, pt: (b, 0, 0))
    gs = pltpu.PrefetchScalarGridSpec(
        num_scalar_prefetch=1,
        grid=(bd, ns),
        in_specs=[per_b((hq, HEAD_DIM))] + [kv_spec(j) for j in range(pp)] * 2 + [lf_spec(j) for j in range(pp)] +
                 [per_b((LANES, HEAD_DIM)), per_b((LANES, HEAD_DIM)), per_b((1, LANES)), per_b((hq, 1))],
        out_specs=per_b((hq, HEAD_DIM)),
        scratch_shapes=[pltpu.VMEM((hq, 1), F32), pltpu.VMEM((hq, 1), F32), pltpu.VMEM((hq, HEAD_DIM), F32),
                        pltpu.VMEM((1, ncol), F32)])
    return pl.pallas_call(
        functools.partial(_fox_sample_kernel, pp=pp, t=t), grid_spec=gs,
        out_shape=jax.ShapeDtypeStruct((bd, hq, HEAD_DIM), BF16),
        compiler_params=_cparams(2, 48),
        name="fox_sample_attention",
    )(page_table, q_rows, *([cache_k] * pp), *([cache_v] * pp), *([lf_rows] * pp), kn_pad, vn_pad, csn, csq)


def _gdn_kernel(q_ref, k_ref, v_ref, z_ref, small_ref, cum_ref, s0_ref, nw_ref, o_ref, so_ref, s_ref):
    ci = pl.program_id(1)
    c = GDN_CHUNK

    @pl.when(ci == 0)
    def _():
        s_ref[...] = s0_ref[0]

    cum = cum_ref[0]
    cum_t = cum.T
    small = small_ref[0]
    r = lax.broadcasted_iota(jnp.int32, (c, c), 0)
    cc = lax.broadcasted_iota(jnp.int32, (c, c), 1)
    tri = cc <= r
    strict = cc < r
    eye = (cc == r).astype(F32)
    pair_masks = []
    shift = 0
    while (1 << shift) < c:
        rb = r >> shift
        pair_masks.append(((rb & 1) == 1) & ((cc >> shift) == rb - 1))
        shift += 1
    heads = range(N_HEADS)
    stack = lambda f: jnp.stack([f(h) for h in heads], axis=0)
    hs = lambda h: slice(h * HEAD_DIM, (h + 1) * HEAD_DIM)
    q = stack(lambda h: q_ref[0, :, hs(h)])
    k = stack(lambda h: k_ref[0, :, hs(h)])
    v = stack(lambda h: v_ref[0, :, hs(h)])
    gc = stack(lambda h: cum[:, N_HEADS + h:N_HEADS + h + 1])
    gr = stack(lambda h: cum_t[N_HEADS + h:N_HEADS + h + 1, :])
    bt = stack(lambda h: small[:, 2 * N_HEADS + h:2 * N_HEADS + h + 1])
    g_last = gc[:, c - 1:c, :]
    decay = jnp.where(tri, jnp.exp(jnp.where(tri, gc - gr, 0.0)), 0.0)
    kb = k * bt
    kk_qk = _bmm_nt(jnp.concatenate([kb, q], axis=1).astype(BF16), k.astype(BF16))
    lmat = jnp.where(strict, kk_qk[:, :c] * decay, 0.0)
    a_in = jnp.where(tri, kk_qk[:, c:] * decay, 0.0)
    tm = jnp.broadcast_to(eye, (N_HEADS, c, c))
    for e_mask in pair_masks:
        tb = tm.astype(BF16)
        te = _bmm(tb, jnp.where(e_mask, lmat, 0.0).astype(BF16))
        tm = tm - _bmm(te.astype(BF16), tb)
    eg = jnp.exp(gc)
    uwb = _bmm(tm.astype(BF16), jnp.concatenate([v * bt, kb * eg], axis=2).astype(BF16)).astype(BF16)
    au = _bmm(a_in.astype(BF16), uwb)
    kt = k * jnp.exp(g_last - gc)
    kt_t = stack(lambda h: kt[h].T)
    ku = _bmm(kt_t.astype(BF16), uwb)
    qt = q * eg - au[:, :, HEAD_DIM:]
    s_all = s_ref[...]
    x = _bmm(jnp.concatenate([-ku[:, :, HEAD_DIM:], qt], axis=1).astype(BF16), s_all.astype(BF16))
    s_new = s_all * jnp.exp(g_last) + x[:, :HEAD_DIM] + ku[:, :, :HEAD_DIM]
    s_ref[...] = s_new
    so_ref[0] = s_new
    o = x[:, HEAD_DIM:] + au[:, :, :HEAD_DIM]
    o = o * lax.rsqrt(jnp.mean(o * o, axis=-1, keepdims=True) + RMS_EPS) * nw_ref[...]
    for h in heads:
        zh = z_ref[0, :, hs(h)]
        o_ref[0, :, hs(h)] = (o[h] * (zh * _sigmoid(zh))).astype(BF16)


def gdn_chunks(q, k, v, z, small, cum, s0, norm_w):
    nseq, tp, _ = q.shape
    c = GDN_CHUNK
    blk = lambda w: pl.BlockSpec((1, c, w), lambda s, i: (s, i, 0))
    st = pl.BlockSpec((1, N_HEADS, HEAD_DIM, HEAD_DIM), lambda s, i: (s, 0, 0, 0))
    return pl.pallas_call(
        _gdn_kernel,
        grid=(nseq, tp // c),
        in_specs=[blk(D_HEADS)] * 4 + [blk(LANES)] * 2 + [st, pl.BlockSpec((1, HEAD_DIM), lambda s, i: (0, 0))],
        out_specs=[blk(D_HEADS), st],
        out_shape=[jax.ShapeDtypeStruct((nseq, tp, D_HEADS), BF16),
                   jax.ShapeDtypeStruct((nseq, N_HEADS, HEAD_DIM, HEAD_DIM), F32)],
        scratch_shapes=[pltpu.VMEM((N_HEADS, HEAD_DIM, HEAD_DIM), F32)],
        compiler_params=_cparams(2, 32),
        name="gdn_chunks",
    )(q, k, v, z, small, cum, s0, norm_w)


def _split3(a):
    hi = a.astype(BF16)
    lo = (a - hi.astype(F32)).astype(BF16)
    return hi, lo


def _out_proj_kernel(x_ref, of_ref, og_ref, g1_ref, sc_ref, sh_ref, rw_ref, wt_ref, wb_ref, wr_ref, br_ref,
                     x1_ref, h2_ref, lg_ref):
    g, tm, d = x_ref.shape
    m = g * tm
    mix = _dot(of_ref[...].reshape(m, D_HEADS), wt_ref[...]) + _dot(og_ref[...].reshape(m, D_HEADS), wb_ref[...])
    x1 = x_ref[...] + g1_ref[...] * mix.reshape(g, tm, d)
    x1_ref[...] = x1
    ms = jnp.mean(x1 * x1, axis=-1, keepdims=True)
    h2 = x1 * lax.rsqrt(ms + RMS_EPS) * rw_ref[...]
    h2 = (h2 * (1.0 + sc_ref[...]) + sh_ref[...]).reshape(m, d)
    h2_ref[...] = h2.reshape(g, tm, d)
    hh, hl = _split3(h2)
    wh, wl = _split3(wr_ref[...])
    lg = _dot(hh, wh) + (_dot(hh, wl) + _dot(hl, wh)) + br_ref[...]
    lg_ref[...] = lg.reshape(g, tm, LANES)


def out_proj(x3, o_fox, o_gdn, g1, sc2, sh2, rms_w, w_top, w_bot, w_router_p, b_router_p, *, group, tm):
    nseq, t, d = x3.shape
    g = group
    blk = lambda w: pl.BlockSpec((g, tm, w), lambda s, i: (s, i, 0))
    const2 = lambda a: pl.BlockSpec(a.shape, lambda s, i: (0, 0))
    res = lambda a: pl.BlockSpec(a.shape, lambda s, i: (0, 0), pipeline_mode=pl.Buffered(1))
    mod = pl.BlockSpec((g, 1, d), lambda s, i: (s, 0, 0))
    return pl.pallas_call(
        _out_proj_kernel,
        grid=(nseq // g, t // tm),
        in_specs=[blk(d), blk(D_HEADS), blk(D_HEADS), mod, mod, mod, const2(rms_w),
                  res(w_top), res(w_bot), const2(w_router_p), const2(b_router_p)],
        out_specs=[blk(d), blk(d), blk(LANES)],
        out_shape=[jax.ShapeDtypeStruct((nseq, t, d), F32), jax.ShapeDtypeStruct((nseq, t, d), F32),
                   jax.ShapeDtypeStruct((nseq, t, LANES), F32)],
        compiler_params=_cparams(2, 40),
        name="out_proj",
    )(x3, o_fox, o_gdn, g1, sc2, sh2, rms_w, w_top, w_bot, w_router_p, b_router_p)


def _moe_dispatch_kernel(tok_ref, nu_ref, src_hbm, o_ref, buf_ref, sem):
    i = pl.program_id(0)
    nu = nu_ref[0]
    rows = buf_ref.shape[1]

    def issue_block(blk):
        slot = blk & 1
        base = blk * rows

        def issue(g, carry):
            for u in range(DISPATCH_UNROLL):
                r = g * DISPATCH_UNROLL + u
                pltpu.make_async_copy(src_hbm.at[pl.ds(tok_ref[base + r], 1)], buf_ref.at[slot, pl.ds(r, 1)],
                                      sem.at[slot]).start(priority=u % 2)
            return carry

        lax.fori_loop(0, rows // DISPATCH_UNROLL, issue, 0)

    @pl.when((i == 0) & (nu > 0))
    def _():
        issue_block(0)

    @pl.when(i + 1 < nu)
    def _():
        issue_block(i + 1)

    @pl.when(i < nu)
    def _():
        slot = i & 1
        pltpu.make_async_copy(src_hbm.at[pl.ds(0, rows)], buf_ref.at[slot], sem.at[slot]).wait()
        o_ref[...] = buf_ref[slot].astype(BF16)

    @pl.when(i >= nu)
    def _():
        o_ref[...] = jnp.zeros(o_ref.shape, o_ref.dtype)


def moe_dispatch(h2_all, row_tok, n_used):
    n, d = h2_all.shape
    r = row_tok.shape[0]
    gs = pltpu.PrefetchScalarGridSpec(
        num_scalar_prefetch=2,
        grid=(r // MOE_ROWS,),
        in_specs=[pl.BlockSpec(memory_space=pl.ANY)],
        out_specs=pl.BlockSpec((MOE_ROWS, d), lambda i, tok, nu: (i, 0)),
        scratch_shapes=[pltpu.VMEM((2, MOE_ROWS, d), F32), pltpu.SemaphoreType.DMA((2,))])
    return pl.pallas_call(
        _moe_dispatch_kernel, grid_spec=gs,
        out_shape=jax.ShapeDtypeStruct((r, d), BF16),
        compiler_params=_cparams(1, 16),
        name="moe_dispatch",
    )(row_tok, n_used, h2_all)


def _run_weights(i, col_offsets, meta, w_hbm, wf_ref, wb_refs, sem):
    be_ref, first_ref, par_ref, nxt_ref = meta
    tn = wf_ref.shape[3]

    def copies(e, slot):
        return [pltpu.make_async_copy(w_hbm.at[e, :, pl.ds(pl.multiple_of(c0, tn), tn)], wf_ref.at[slot, k],
                                      sem.at[slot]) for k, c0 in enumerate(col_offsets)]

    @pl.when(first_ref[i] == 1)
    def _():
        slot = par_ref[i]

        @pl.when(i == 0)
        def _():
            for cp in copies(be_ref[i], slot):
                cp.start()

        for cp in copies(be_ref[i], slot):
            cp.wait()
        for k, wb_ref in enumerate(wb_refs):
            wb_ref[...] = wf_ref[slot, k].astype(BF16)

        @pl.when(nxt_ref[i] >= 0)
        def _():
            for cp in copies(nxt_ref[i], 1 - slot):
                cp.start()


def _moe_gu_kernel(be_ref, first_ref, par_ref, nxt_ref, nu_ref, x_ref, w_hbm, bg_ref, bl_ref, h_ref,
                   wf_ref, wgb_ref, wlb_ref, sem):
    j = pl.program_id(0)
    i = pl.program_id(1)
    tn = h_ref.shape[1]
    nj = pl.num_programs(0)

    @pl.when(i < nu_ref[0])
    def _():
        _run_weights(i, (j * tn, (nj + j) * tn), (be_ref, first_ref, par_ref, nxt_ref), w_hbm, wf_ref,
                     (wgb_ref, wlb_ref), sem)
        x = x_ref[...]
        glu = jnp.minimum(_dot(x, wgb_ref[...]) + bg_ref[0], SWIGLU_LIMIT)
        lin = jnp.clip(_dot(x, wlb_ref[...]) + bl_ref[0], -SWIGLU_LIMIT, SWIGLU_LIMIT)
        h_ref[...] = (glu * _sigmoid(SWIGLU_ALPHA * glu) * (lin + 1.0)).astype(BF16)

    @pl.when(i >= nu_ref[0])
    def _():
        h_ref[...] = jnp.zeros(h_ref.shape, h_ref.dtype)


def _moe_specs(rows, d_in, tn, n_bias, nj):
    blk_i = lambda i, nu: jnp.minimum(i, nu[0] - 1)
    x_spec = pl.BlockSpec((rows, d_in), lambda j, i, be, fi, pa, nx, nu: (blk_i(i, nu), 0))
    bias = [pl.BlockSpec((1, 1, tn), (lambda j, i, be, fi, pa, nx, nu, k=k: (be[blk_i(i, nu)], 0, k * nj + j)))
            for k in range(n_bias)]
    out_spec = pl.BlockSpec((rows, tn), lambda j, i, be, fi, pa, nx, nu: (i, j))
    return x_spec, bias, out_spec


def moe_gate_up(x_sorted, meta, n_used, w_gu, b_gu3, *, tn):
    r, d = x_sorted.shape
    d_ff = w_gu.shape[2] // 2
    nj = d_ff // tn
    x_spec, bias, out_spec = _moe_specs(MOE_ROWS, d, tn, 2, nj)
    gs = pltpu.PrefetchScalarGridSpec(
        num_scalar_prefetch=5,
        grid=(nj, r // MOE_ROWS),
        in_specs=[x_spec, pl.BlockSpec(memory_space=pl.ANY)] + bias,
        out_specs=out_spec,
        scratch_shapes=[pltpu.VMEM((2, 2, d, tn), F32), pltpu.VMEM((d, tn), BF16), pltpu.VMEM((d, tn), BF16),
                        pltpu.SemaphoreType.DMA((2,))])
    return pl.pallas_call(
        _moe_gu_kernel, grid_spec=gs,
        out_shape=jax.ShapeDtypeStruct((r, d_ff), BF16),
        compiler_params=_cparams(2, 52),
        name="moe_gate_up",
    )(*meta, n_used, x_sorted, w_gu, b_gu3, b_gu3)


def _moe_down_kernel(be_ref, first_ref, par_ref, nxt_ref, nu_ref, h_ref, w_hbm, b_ref, o_ref, wf_ref, wb_ref, sem):
    j = pl.program_id(0)
    i = pl.program_id(1)
    tn = o_ref.shape[1]

    @pl.when(i < nu_ref[0])
    def _():
        _run_weights(i, (j * tn,), (be_ref, first_ref, par_ref, nxt_ref), w_hbm, wf_ref, (wb_ref,), sem)
        o_ref[...] = _dot(h_ref[...], wb_ref[...]) + b_ref[0]

    @pl.when(i >= nu_ref[0])
    def _():
        o_ref[...] = jnp.zeros(o_ref.shape, o_ref.dtype)


def moe_down(h_sorted, meta, n_used, w_down, b_down3, *, tn):
    r, d_ff = h_sorted.shape
    d = w_down.shape[2]
    x_spec, bias, out_spec = _moe_specs(MOE_ROWS, d_ff, tn, 1, d // tn)
    gs = pltpu.PrefetchScalarGridSpec(
        num_scalar_prefetch=5,
        grid=(d // tn, r // MOE_ROWS),
        in_specs=[x_spec, pl.BlockSpec(memory_space=pl.ANY)] + bias,
        out_specs=out_spec,
        scratch_shapes=[pltpu.VMEM((2, 1, d_ff, tn), F32), pltpu.VMEM((d_ff, tn), BF16),
                        pltpu.SemaphoreType.DMA((2,))])
    return pl.pallas_call(
        _moe_down_kernel, grid_spec=gs,
        out_shape=jax.ShapeDtypeStruct((r, d), F32),
        compiler_params=_cparams(2, 40),
        name="moe_down",
    )(*meta, n_used, h_sorted, w_down, b_down3)


def _route(logits):
    n = logits.shape[0]
    a = n * TOP_K
    top_val, top_idx = lax.top_k(logits, TOP_K)
    gate = jax.nn.softmax(top_val, axis=-1)
    flat_e = top_idx.reshape(a).astype(jnp.int32)
    experts = jnp.arange(N_EXPERTS, dtype=jnp.int32)
    order = jnp.argsort(flat_e).astype(jnp.int32)
    counts = jnp.sum((flat_e[:, None] == experts[None, :]).astype(jnp.int32), axis=0)
    starts = jnp.cumsum(counts) - counts
    padded = -(-counts // MOE_ROWS) * MOE_ROWS
    pad_ends = jnp.cumsum(padded)
    pad_starts = pad_ends - padded
    sorted_e = flat_e[order]
    dest = pad_starts[sorted_e] + jnp.arange(a, dtype=jnp.int32) - starts[sorted_e]
    pos = dest[jnp.argsort(order)].reshape(n, TOP_K)
    nb = -(-(a + N_EXPERTS * (MOE_ROWS - 1)) // MOE_ROWS)
    rows = jnp.arange(nb * MOE_ROWS, dtype=jnp.int32)
    row_e = jnp.minimum(jnp.sum((pad_ends[None, :] <= rows[:, None]).astype(jnp.int32), axis=1), N_EXPERTS - 1)
    off = rows - pad_starts[row_e]
    src = jnp.clip(starts[row_e] + off, 0, a - 1)
    row_tok = jnp.where(off < counts[row_e], order[src] // TOP_K, 0).astype(jnp.int32)
    n_used = pad_ends[-1] // MOE_ROWS
    blocks = jnp.arange(nb, dtype=jnp.int32)
    block_e = row_e[::MOE_ROWS]
    used = blocks < n_used
    first = used & ((blocks == 0) | (block_e != jnp.roll(block_e, 1)))
    parity = (jnp.cumsum(first.astype(jnp.int32)) - 1) & 1
    later_first = first[None, :] & (blocks[None, :] > blocks[:, None])
    next_first = jnp.min(jnp.where(later_first, blocks[None, :], nb), axis=1)
    next_e = jnp.where(next_first < nb, block_e[jnp.minimum(next_first, nb - 1)], -1)
    meta = (block_e.astype(jnp.int32), first.astype(jnp.int32), parity.astype(jnp.int32), next_e.astype(jnp.int32))
    return row_tok, gate, meta, n_used.astype(jnp.int32).reshape(1), pos


def _forward(x_prompt, x_sample, cache_k, cache_v, cache_logf, state_gdn, state_conv, page_table,
             c_prompt, c_sample, w_ada, b_ada, rms_mix, rms_ffn, w_in, b_forget, q_norm, k_norm,
             conv_w, a_log, dt_bias, gdn_norm, w_out, w_router, b_router, w_gu, b_gu, w_down, b_down):
    depth = w_ada.shape[0]
    assert depth == 1, "single-layer trunk"
    l = 0
    bp, s, d = x_prompt.shape
    bd, t, _ = x_sample.shape
    n_pool = cache_k.shape[1]
    tm_p = min(PROJ_ROWS, s)

    n_c = bp + bd
    c_all = jnp.concatenate([c_prompt, c_sample, jnp.zeros((-n_c % SUBLANES, d), F32)], axis=0)
    mod = ada_mod(c_all, w_ada[l], b_ada[l])
    modp = mod[:bp].reshape(bp, 1, N_ADA, d)
    mods = mod[bp:n_c].reshape(bd, 1, N_ADA, d)

    w = w_in[l]
    o_ff, o_g, o_ab = 3 * D_HEADS, 3 * D_HEADS + N_HEADS, 7 * D_HEADS + N_HEADS
    w_small = jnp.concatenate([w[:, o_ff:o_g], w[:, o_ab:o_ab + 2 * N_HEADS]], axis=1)
    w_small = jnp.pad(w_small, ((0, 0), (0, LANES - 3 * N_HEADS)))
    w_fox = jnp.concatenate([w[:, :o_ff], w_small], axis=1).astype(BF16)
    w_g = w[:, o_g:o_ab].astype(BF16)
    qn_t = jnp.tile(q_norm[l], N_HEADS).reshape(1, D_HEADS)
    kn_t = jnp.tile(k_norm[l], N_HEADS).reshape(1, D_HEADS)
    pvec = jnp.zeros((SUBLANES, LANES), F32)
    pvec = pvec.at[0, 0:N_HEADS].set(b_forget[l]).at[0, N_HEADS:2 * N_HEADS].set(dt_bias[l])
    pvec = pvec.at[1, N_HEADS:2 * N_HEADS].set(a_log[l])
    rw1 = rms_mix[l].reshape(1, d)
    rw2 = rms_ffn[l].reshape(1, d)
    w_top = w_out[l][:D_HEADS].astype(BF16)
    w_bot = w_out[l][D_HEADS:].astype(BF16)
    w_router_p = jnp.pad(w_router[l], ((0, 0), (0, LANES - N_EXPERTS)))
    b_router_p = jnp.pad(b_router[l], (0, LANES - N_EXPERTS)).reshape(1, LANES)
    nw = gdn_norm[l].reshape(1, HEAD_DIM)

    def mixer_inputs(x3, m4, conv_in, group, tm):
        sh1, sc1 = m4[:, :, 0], m4[:, :, 1]
        pf = proj_fox(x3, sc1, sh1, rw1, w_fox, qn_t, kn_t, pvec, group=group, tm=tm)
        pg = proj_gdn(x3, sc1, sh1, rw1, w_g, conv_w[l], conv_in, group=group, tm=tm)
        return pf, pg

    pf, pg = mixer_inputs(x_prompt, modp, jnp.zeros((bp, SUBLANES, 3 * D_HEADS), F32), 1, tm_p)
    qb_p, k_p, v_p, kb_p, vb_p, small_p, cum_p = pf
    gq_p, gk_p, gv_p, gz_p, cst_p = pg
    f_t = cum_p[..., :N_HEADS].transpose(0, 2, 1).reshape(bp * N_HEADS, s)
    o_fox_p = fox_prompt_attention(qb_p, kb_p, vb_p, f_t.reshape(bp * N_HEADS, s, 1),
                                   f_t.reshape(bp * N_HEADS, 1, s), tq=min(ATTN_TQ, s))
    o_gdn_p, s_p = gdn_chunks(gq_p, gk_p, gv_p, gz_p, small_p, cum_p,
                              jnp.zeros((bp, N_HEADS, HEAD_DIM, HEAD_DIM), F32), nw)
    x1_p, h2_p, lg_p = out_proj(x_prompt, o_fox_p, o_gdn_p, modp[:, :, 2], modp[:, :, 4], modp[:, :, 3],
                                rw2, w_top, w_bot, w_router_p, b_router_p, group=1, tm=tm_p)

    conv_in = jnp.pad(state_conv[l], ((0, 0), (SUBLANES - (CONV_W - 1), 0), (0, 0)))
    sf, sg = mixer_inputs(x_sample, mods, conv_in, bd, t)
    qb_s, k_s, v_s, kb_s, vb_s, small_s, cum_s = sf
    gq_s, gk_s, gv_s, gz_s, cst_s = sg
    hq = N_HEADS * t
    cs = cum_s[..., :N_HEADS]
    csn = jnp.pad(cs.reshape(bd, 1, hq), ((0, 0), (0, 0), (0, LANES - hq)))
    csq = cs.transpose(0, 2, 1).reshape(bd, hq, 1)
    q_rows = qb_s.reshape(bd, t, N_HEADS, HEAD_DIM).transpose(0, 2, 1, 3).reshape(bd, hq, HEAD_DIM)
    new_rows = lambda a: jnp.pad(a.reshape(bd, hq, HEAD_DIM), ((0, 0), (0, LANES - hq), (0, 0)))
    lf_rows = cache_logf[l].reshape(n_pool, 1, PAGE_SIZE * N_HEADS)
    o_rows = fox_sample_attention(q_rows, cache_k, cache_v, lf_rows, page_table, new_rows(kb_s), new_rows(vb_s),
                                  csn, csq, pp=SAMPLE_PAGES, t=t)
    o_fox_s = o_rows.reshape(bd, N_HEADS, t, HEAD_DIM).transpose(0, 2, 1, 3).reshape(bd, t, D_HEADS)
    padc = lambda a: jnp.pad(a, ((0, 0), (0, GDN_CHUNK - t), (0, 0)))
    cum_s_pad = jnp.pad(cum_s, ((0, 0), (0, GDN_CHUNK - t), (0, 0)), mode="edge")
    o_gdn_s, s_s = gdn_chunks(padc(gq_s), padc(gk_s), padc(gv_s), padc(gz_s), padc(small_s), cum_s_pad,
                              state_gdn[l].astype(F32), nw)
    x1_s, h2_s, lg_s = out_proj(x_sample, o_fox_s, o_gdn_s[:, :t], mods[:, :, 2], mods[:, :, 4], mods[:, :, 3],
                                rw2, w_top, w_bot, w_router_p, b_router_p, group=bd, tm=t)

    n_p, n_s = bp * s, bd * t
    h2_all = jnp.concatenate([h2_p.reshape(n_p, d), h2_s.reshape(n_s, d)], axis=0)
    logits = jnp.concatenate([lg_p.reshape(n_p, LANES), lg_s.reshape(n_s, LANES)], axis=0)[:, :N_EXPERTS]
    row_tok, gate, meta, n_used, pos = _route(logits)
    x_sorted = moe_dispatch(h2_all, row_tok, n_used)
    hid = moe_gate_up(x_sorted, meta, n_used, w_gu[l], b_gu[l].reshape(N_EXPERTS, 1, -1), tn=MOE_GU_TN)
    outg = moe_down(hid, meta, n_used, w_down[l], b_down[l].reshape(N_EXPERTS, 1, -1), tn=MOE_DOWN_TN)
    moe = sum(gate[:, k:k + 1] * outg[pos[:, k]] for k in range(TOP_K))
    y_p = x1_p + modp[:, :, 5] * moe[:n_p].reshape(bp, s, d)
    y_s = x1_s + mods[:, :, 5] * moe[n_p:].reshape(bd, t, d)

    hd = lambda a, b_, t_: a.reshape(1, b_, t_, N_HEADS, HEAD_DIM)
    return (y_p, y_s,
            hd(k_p, bp, s), hd(v_p, bp, s), small_p[..., :N_HEADS][None],
            s_p[None], cst_p[:, SUBLANES - (CONV_W - 1):][None],
            hd(k_s, bd, t), hd(v_s, bd, t), small_s[..., :N_HEADS][None],
            s_s[None], cst_s[:, SUBLANES - (CONV_W - 1):][None])


def kernel(x_prompt, x_sample, cache_k, cache_v, cache_logf, state_gdn, state_conv, page_table, c_prompt, c_sample, w_ada, b_ada, rms_mix, rms_ffn, w_in, b_forget, q_norm, k_norm, conv_w, a_log, dt_bias, gdn_norm, w_out, w_router, b_router, w_gu, b_gu, w_down, b_down):
    return _forward(x_prompt, x_sample, cache_k, cache_v, cache_logf, state_gdn, state_conv, page_table,
                    c_prompt, c_sample, w_ada, b_ada, rms_mix, rms_ffn, w_in, b_forget, q_norm, k_norm,
                    conv_w, a_log, dt_bias, gdn_norm, w_out, w_router, b_router, w_gu, b_gu, w_down, b_down)
```
